```python
import math, functools
import jax, jax.numpy as jnp
from jax import lax
import numpy as np

D_MODEL = 4096
BATCH = 1
SEQ = 8192
DEPTH = 1
DEC_BATCH = 32
DEC_SEQ = 4
PAST_LEN = 8192
PAGE_SIZE = 128

DA_DK = 128
DA_DV = 2 * DA_DK
DA_HEADS = D_MODEL // (2 * DA_DV)
DA_WIDTH = DA_HEADS * DA_DV
Q_W = 2 * DA_HEADS * DA_DK
K_W = 2 * DA_HEADS * DA_DK
DA_COLS = Q_W + K_W + DA_WIDTH
RW_HEAD = 64
RW_WIDTH = D_MODEL - DA_WIDTH
RW_HEADS = RW_WIDTH // RW_HEAD
D_DECAY = max(32, round(1.8 * math.sqrt(D_MODEL) / 32) * 32)
D_AAA = max(32, round(1.8 * math.sqrt(D_MODEL) / 32) * 32)
D_GATE = max(32, round(0.6 * D_MODEL ** 0.8 / 32) * 32)
RW_COLS = 3 * RW_WIDTH + D_DECAY + D_AAA + D_GATE
IN_COLS = DA_COLS + RW_COLS
D_FF = 4 * D_MODEL
PLE_DIM = 256
ROPE_THETA = 10000.0
Q_BLOCK = 128
NORM_EPS = 1e-6
SUBLN_EPS = 1e-5
RW_LN_EPS = 64e-5
NEG_INF = -1e30

kernel_name = "hymba_diffattn_rwkv7_decode_step"


def rmsnorm(x, g, eps=NORM_EPS):
    xf = x.astype(jnp.float32)
    y = xf * lax.rsqrt(jnp.mean(xf * xf, axis=-1, keepdims=True) + eps)
    return (y * g.astype(jnp.float32)).astype(x.dtype)


def rope(x, pos):
    half = x.shape[-1] // 2
    inv = ROPE_THETA ** (-jnp.arange(half, dtype=jnp.float32) / half)
    ang = pos.astype(jnp.float32)[:, None] * inv[None, :]
    cos = jnp.cos(ang)[:, None, :]
    sin = jnp.sin(ang)[:, None, :]
    xf = x.astype(jnp.float32)
    x1, x2 = xf[..., :half], xf[..., half:]
    return jnp.concatenate([x1 * cos - x2 * sin, x2 * cos + x1 * sin], axis=-1).astype(x.dtype)


def prompt_attention(q, k, v, lam):
    b, s = q.shape[:2]
    nb = s // Q_BLOCK
    qb = jnp.moveaxis(q.reshape(b, nb, Q_BLOCK, DA_HEADS, 2, DA_DK), 1, 0)
    starts = jnp.arange(nb, dtype=jnp.int32) * Q_BLOCK
    kpos = jnp.arange(s, dtype=jnp.int32)
    scale = DA_DK ** -0.5

    def one_block(args):
        qblk, start = args
        sc = jnp.einsum('bqhmd,bkhmd->bhmqk', qblk, k, preferred_element_type=jnp.float32) * scale
        qpos = start + jnp.arange(Q_BLOCK, dtype=jnp.int32)
        sc = jnp.where(kpos[None, :] <= qpos[:, None], sc, NEG_INF)
        p = jax.nn.softmax(sc, axis=-1)
        a = p[:, :, 0] - lam * p[:, :, 1]
        return jnp.einsum('bhqk,bkhe->bqhe', a.astype(v.dtype), v, preferred_element_type=jnp.float32)

    o = lax.map(one_block, (qb, starts))
    return jnp.moveaxis(o, 0, 1).reshape(b, s, DA_HEADS, DA_DV)


def sample_attention(q, k, v, lam, k_past, v_past):
    t = q.shape[1]
    p_len = k_past.shape[1]
    scale = DA_DK ** -0.5
    s_past = jnp.einsum('bqhmd,bkhmd->bhmqk', q, k_past, preferred_element_type=jnp.float32) * scale
    s_new = jnp.einsum('bqhmd,bkhmd->bhmqk', q, k, preferred_element_type=jnp.float32) * scale
    causal = jnp.arange(t)[None, :] <= jnp.arange(t)[:, None]
    s_new = jnp.where(causal, s_new, NEG_INF)
    p = jax.nn.softmax(jnp.concatenate([s_past, s_new], axis=-1), axis=-1)
    a = p[:, :, 0] - lam * p[:, :, 1]
    o = jnp.einsum('bhqk,bkhe->bqhe', a[..., :p_len].astype(v_past.dtype), v_past,
                   preferred_element_type=jnp.float32)
    o = o + jnp.einsum('bhqk,bkhe->bqhe', a[..., p_len:].astype(v.dtype), v,
                       preferred_element_type=jnp.float32)
    return o


def rwkv7_mix(z, z_prev, s0, mu, w0, w2, a0, a2, g2, k_k, k_a, r_k, ln_g, ln_b):
    b, t, _ = z.shape
    z_shift = jnp.concatenate([z_prev[:, None, :].astype(z.dtype), z[:, :-1]], axis=1)
    zz = (z + (z_shift - z) * mu).astype(jnp.float32)
    r, kr, v, zw, za, zg = jnp.split(
        zz, [RW_WIDTH, 2 * RW_WIDTH, 3 * RW_WIDTH, 3 * RW_WIDTH + D_DECAY,
             3 * RW_WIDTH + D_DECAY + D_AAA], axis=-1)
    w_log = -jax.nn.softplus(-(w0 + jnp.tanh(zw) @ w2)) - 0.5
    decay = jnp.exp(-jnp.exp(w_log))
    a = jax.nn.sigmoid(a0 + za @ a2)
    g = jax.nn.sigmoid(zg) @ g2
    heads = lambda u: u.reshape(b, t, RW_HEADS, RW_HEAD)
    kk = heads(kr * k_k)
    kk = kk * lax.rsqrt(jnp.maximum(jnp.sum(kk * kk, axis=-1, keepdims=True), 1e-24))
    k = heads(kr * (1.0 + (a - 1.0) * k_a))
    r_h, v_h, w_h, a_h = heads(r), heads(v), heads(decay), heads(a)
    b_h = kk * a_h

    def step(state, inp):
        r_t, w_t, k_t, v_t, kk_t, b_t = inp
        sa = jnp.einsum('bhij,bhj->bhi', state, -kk_t)
        state = (state * w_t[:, :, None, :] + sa[..., None] * b_t[:, :, None, :]
                 + v_t[..., None] * k_t[:, :, None, :])
        y_t = jnp.einsum('bhij,bhj->bhi', state, r_t)
        return state, y_t

    xs = tuple(jnp.moveaxis(u, 1, 0) for u in (r_h, w_h, k, v_h, kk, b_h))
    s_final, y = lax.scan(step, s0.astype(jnp.float32), xs)
    y = jnp.moveaxis(y, 0, 1)
    mean = jnp.mean(y, axis=-1, keepdims=True)
    var = jnp.mean(jnp.square(y - mean), axis=-1, keepdims=True)
    y = ((y - mean) * lax.rsqrt(var + RW_LN_EPS)).reshape(b, t, RW_WIDTH) * ln_g + ln_b
    bonus = (jnp.sum(r_h * k * r_k, axis=-1, keepdims=True) * v_h).reshape(b, t, RW_WIDTH)
    return (y + bonus) * g, s_final, z[:, -1]


def decoder_layer(x, pe, pos, z_prev, s0, attend, lam_init, lw):
    b, t, _ = x.shape
    h = rmsnorm(x, lw['g_pre_mix'])
    zin = h @ lw['w_in']
    q = rope(zin[..., :Q_W].reshape(b, t, 2 * DA_HEADS, DA_DK), pos)
    k = rope(zin[..., Q_W:Q_W + K_W].reshape(b, t, 2 * DA_HEADS, DA_DK), pos)
    v = zin[..., Q_W + K_W:DA_COLS].reshape(b, t, DA_HEADS, DA_DV)
    lam = (jnp.exp(jnp.sum(lw['lam_q1'].astype(jnp.float32) * lw['lam_k1'].astype(jnp.float32)))
           - jnp.exp(jnp.sum(lw['lam_q2'].astype(jnp.float32) * lw['lam_k2'].astype(jnp.float32)))
           + lam_init)
    o = attend(q.reshape(b, t, DA_HEADS, 2, DA_DK), k.reshape(b, t, DA_HEADS, 2, DA_DK), v, lam)
    o = (rmsnorm(o, lw['da_subln_g'], SUBLN_EPS) * (1.0 - lam_init)).reshape(b, t, DA_WIDTH)
    rw, s_new, z_last = rwkv7_mix(zin[..., DA_COLS:], z_prev, s0, lw['rw_mu'], lw['rw_w0'],
                                  lw['rw_w2'], lw['rw_a0'], lw['rw_a2'], lw['rw_g2'],
                                  lw['rw_k_k'], lw['rw_k_a'], lw['rw_r_k'], lw['rw_ln_g'],
                                  lw['rw_ln_b'])
    mix = jnp.concatenate([o, rw], axis=-1).astype(x.dtype) @ lw['w_out']
    x = x + rmsnorm(mix, lw['g_post_mix'])
    f = jnp.square(jax.nn.relu(rmsnorm(x, lw['g_pre_ffn']) @ lw['w_up'])) @ lw['w_down']
    x = x + rmsnorm(f, lw['g_post_ffn'])
    gate = jax.nn.sigmoid(rmsnorm(x, lw['g_pre_ple']) @ lw['w_ple_gate'])
    x = x + rmsnorm(gate * (pe @ lw['w_ple_proj']), lw['g_post_ple'])
    return x, k, v, s_new, z_last


def setup_inputs(seed: int = 0) -> dict:
    key = jax.random.key(seed)
    ks = iter(jax.random.split(key, 48))
    f32 = jnp.float32
    n_pages = PAST_LEN // PAGE_SIZE
    n_used = DEC_BATCH * n_pages
    n_pool = n_used + max(1, n_used // 4)

    def nrm(shape, scale):
        return jax.random.normal(next(ks), shape, f32) * scale

    def gain(shape):
        return 1.0 + nrm(shape, 0.05)

    x_prompt = nrm((BATCH, SEQ, D_MODEL), 1.0)
    x_sample = nrm((DEC_BATCH, DEC_SEQ, D_MODEL), 1.0)
    cache_k = nrm((DEPTH, n_pool, PAGE_SIZE, 2 * DA_HEADS, DA_DK), 1.0)
    cache_v = nrm((DEPTH, n_pool, PAGE_SIZE, DA_HEADS, DA_DV), 1.0)
    state_wkv = nrm((DEPTH, DEC_BATCH, RW_HEADS, RW_HEAD, RW_HEAD), 0.5)
    state_shift = nrm((DEPTH, DEC_BATCH, RW_COLS), 1.0)
    page_table = jax.random.permutation(next(ks), n_pool)[:n_used].reshape(
        DEC_BATCH, n_pages).astype(jnp.int32)
    p_prompt = nrm((DEPTH, BATCH, SEQ, PLE_DIM), 1.0)
    p_sample = nrm((DEPTH, DEC_BATCH, DEC_SEQ, PLE_DIM), 1.0)
    return {
        'x_prompt': x_prompt,
        'x_sample': x_sample,
        'cache_k': cache_k,
        'cache_v': cache_v,
        'state_wkv': state_wkv,
        'state_shift': state_shift,
        'page_table': page_table,
        'p_prompt': p_prompt,
        'p_sample': p_sample,
        'g_pre_mix': gain((DEPTH, D_MODEL)),
        'w_in': nrm((DEPTH, D_MODEL, IN_COLS), D_MODEL ** -0.5),
        'lam_q1': nrm((DEPTH, DA_DK), 0.1),
        'lam_k1': nrm((DEPTH, DA_DK), 0.1),
        'lam_q2': nrm((DEPTH, DA_DK), 0.1),
        'lam_k2': nrm((DEPTH, DA_DK), 0.1),
        'da_subln_g': gain((DEPTH, DA_DV)),
        'rw_mu': jax.random.uniform(next(ks), (DEPTH, RW_COLS), f32, 0.0, 1.0),
        'rw_w0': jax.random.uniform(next(ks), (DEPTH, RW_WIDTH), f32, -6.5, -1.5),
        'rw_w2': nrm((DEPTH, D_DECAY, RW_WIDTH), 0.1 * D_DECAY ** -0.5),
        'rw_a0': nrm((DEPTH, RW_WIDTH), 0.1),
        'rw_a2': nrm((DEPTH, D_AAA, RW_WIDTH), 0.1 * D_AAA ** -0.5),
        'rw_g2': nrm((DEPTH, D_GATE, RW_WIDTH), D_GATE ** -0.5),
        'rw_k_k': 0.85 + nrm((DEPTH, RW_WIDTH), 0.05),
        'rw_k_a': 1.0 + nrm((DEPTH, RW_WIDTH), 0.05),
        'rw_r_k': nrm((DEPTH, RW_HEADS, RW_HEAD), 0.1),
        'rw_ln_g': gain((DEPTH, RW_WIDTH)),
        'rw_ln_b': nrm((DEPTH, RW_WIDTH), 0.01),
        'w_out': nrm((DEPTH, D_MODEL, D_MODEL), D_MODEL ** -0.5),
        'g_post_mix': gain((DEPTH, D_MODEL)),
        'g_pre_ffn': gain((DEPTH, D_MODEL)),
        'w_up': nrm((DEPTH, D_MODEL, D_FF), D_MODEL ** -0.5),
        'w_down': nrm((DEPTH, D_FF, D_MODEL), D_FF ** -0.5),
        'g_post_ffn': gain((DEPTH, D_MODEL)),
        'g_pre_ple': gain((DEPTH, D_MODEL)),
        'w_ple_gate': nrm((DEPTH, D_MODEL, D_MODEL), D_MODEL ** -0.5),
        'w_ple_proj': nrm((DEPTH, PLE_DIM, D_MODEL), PLE_DIM ** -0.5),
        'g_post_ple': gain((DEPTH, D_MODEL)),
    }


def reference(x_prompt, x_sample, cache_k, cache_v, state_wkv, state_shift, page_table,
              p_prompt, p_sample, g_pre_mix, w_in, lam_q1, lam_k1, lam_q2, lam_k2, da_subln_g,
              rw_mu, rw_w0, rw_w2, rw_a0, rw_a2, rw_g2, rw_k_k, rw_k_a, rw_r_k, rw_ln_g,
              rw_ln_b, w_out, g_post_mix, g_pre_ffn, w_up, w_down, g_post_ffn, g_pre_ple,
              w_ple_gate, w_ple_proj, g_post_ple):
    db, n_pages = page_table.shape
    past_len = n_pages * cache_k.shape[2]
    b = x_prompt.shape[0]
    pos_prompt = jnp.arange(x_prompt.shape[1], dtype=jnp.int32)
    pos_sample = past_len + jnp.arange(x_sample.shape[1], dtype=jnp.int32)
    shift0 = jnp.zeros((b, RW_COLS), x_prompt.dtype)
    wkv0 = jnp.zeros((b, RW_HEADS, RW_HEAD, RW_HEAD), jnp.float32)
    xp, xs = x_prompt, x_sample
    kp_l, vp_l, ks_l, vs_l, wp_l, ws_l, sp_l, ss_l = [], [], [], [], [], [], [], []
    for i in range(DEPTH):
        lam_init = 0.8 - 0.6 * math.exp(-0.3 * i)
        lw = {
            'g_pre_mix': g_pre_mix[i], 'w_in': w_in[i],
            'lam_q1': lam_q1[i], 'lam_k1': lam_k1[i], 'lam_q2': lam_q2[i], 'lam_k2': lam_k2[i],
            'da_subln_g': da_subln_g[i], 'rw_mu': rw_mu[i], 'rw_w0': rw_w0[i], 'rw_w2': rw_w2[i],
            'rw_a0': rw_a0[i], 'rw_a2': rw_a2[i], 'rw_g2': rw_g2[i], 'rw_k_k': rw_k_k[i],
            'rw_k_a': rw_k_a[i], 'rw_r_k': rw_r_k[i], 'rw_ln_g': rw_ln_g[i], 'rw_ln_b': rw_ln_b[i],
            'w_out': w_out[i], 'g_post_mix': g_post_mix[i], 'g_pre_ffn': g_pre_ffn[i],
            'w_up': w_up[i], 'w_down': w_down[i], 'g_post_ffn': g_post_ffn[i],
            'g_pre_ple': g_pre_ple[i], 'w_ple_gate': w_ple_gate[i], 'w_ple_proj': w_ple_proj[i],
            'g_post_ple': g_post_ple[i],
        }
        xp, k_p, v_p, wkv_p, sh_p = decoder_layer(xp, p_prompt[i], pos_prompt, shift0, wkv0,
                                                  prompt_attention, lam_init, lw)
        k_past = cache_k[i][page_table].reshape(db, past_len, DA_HEADS, 2, DA_DK)
        v_past = cache_v[i][page_table].reshape(db, past_len, DA_HEADS, DA_DV)
        attend = functools.partial(sample_attention, k_past=k_past, v_past=v_past)
        xs, k_s, v_s, wkv_s, sh_s = decoder_layer(xs, p_sample[i], pos_sample, state_shift[i],
                                                  state_wkv[i], attend, lam_init, lw)
        kp_l.append(k_p); vp_l.append(v_p); ks_l.append(k_s); vs_l.append(v_s)
        wp_l.append(wkv_p); ws_l.append(wkv_s); sp_l.append(sh_p); ss_l.append(sh_s)
    k_rows_prompt = jnp.stack(kp_l, 0)
    v_rows_prompt = jnp.stack(vp_l, 0)
    k_rows_sample = jnp.stack(ks_l, 0)
    v_rows_sample = jnp.stack(vs_l, 0)
    wkv_prompt = jnp.stack(wp_l, 0)
    wkv_sample = jnp.stack(ws_l, 0)
    shift_prompt = jnp.stack(sp_l, 0)
    shift_sample = jnp.stack(ss_l, 0)
    return (xp, xs, k_rows_prompt, v_rows_prompt, k_rows_sample, v_rows_sample,
            wkv_prompt, wkv_sample, shift_prompt, shift_sample)
```

```python
import functools
import math

import jax
import jax.numpy as jnp
from jax import lax
from jax.experimental import pallas as pl
from jax.experimental.pallas import tpu as pltpu

F32 = jnp.float32
BF16 = jnp.bfloat16

LANES = 128
SUBLANES = 8
VMEM_LIMIT_BYTES = 56 * 1024 * 1024

ROPE_THETA = 10000.0
NORM_EPS = 1e-6
SUBLN_EPS = 1e-5
RW_LN_EPS = 64e-5
NEG_INF = -1e30

RW_N = 64
RW_CHUNK = 64
RW_GROUP = 256
MM_TILE = 1024
ATTN_TILE = 512
PAGES_PER_STEP = 4


def _cparams(sem):
    return pltpu.CompilerParams(dimension_semantics=sem, vmem_limit_bytes=VMEM_LIMIT_BYTES)


def _round_up(x, m):
    return (x + m - 1) // m * m


def _tile(n, pref):
    t = min(n, pref)
    assert n % t == 0, (n, t)
    return t


def _dot(a, b):
    return jnp.dot(a.astype(BF16), b.astype(BF16), preferred_element_type=F32)


def _dot_nt(a, b):
    return lax.dot_general(a.astype(BF16), b.astype(BF16), (((1,), (1,)), ((), ())),
                           preferred_element_type=F32)


def _dot_tn(a, b):
    return lax.dot_general(a.astype(BF16), b.astype(BF16), (((0,), (0,)), ((), ())),
                           preferred_element_type=F32)


def _split3(x):
    hi = x.astype(BF16)
    r1 = x - hi.astype(F32)
    mid = r1.astype(BF16)
    lo = (r1 - mid.astype(F32)).astype(BF16)
    return hi, mid, lo


def _dot_sel(sel, x):
    hi, mid, lo = _split3(x)
    n = x.shape[1]
    y = jnp.dot(sel, jnp.concatenate([hi, mid, lo], axis=1), preferred_element_type=F32)
    return y[:, :n] + y[:, n:2 * n] + y[:, 2 * n:]


def _dot_sel_r(x, sel):
    hi, mid, lo = _split3(x)
    m = x.shape[0]
    y = jnp.dot(jnp.concatenate([hi, mid, lo], axis=0), sel, preferred_element_type=F32)
    return y[:m] + y[m:2 * m] + y[2 * m:]


def _iota(shape, dim):
    return lax.broadcasted_iota(jnp.int32, shape, dim)


def _rmsnorm_kernel(x_ref, g_ref, o_ref, *, eps):
    x = x_ref[...]
    y = x * lax.rsqrt(jnp.mean(x * x, axis=-1, keepdims=True) + eps)
    o_ref[...] = (y * g_ref[...]).astype(o_ref.dtype)


def _rmsnorm(x, g, eps):
    m, d = x.shape
    tm = _tile(m, 256)
    return pl.pallas_call(
        functools.partial(_rmsnorm_kernel, eps=eps),
        grid=(m // tm,),
        in_specs=[pl.BlockSpec((tm, d), lambda i: (i, 0)), pl.BlockSpec((1, d), lambda i: (0, 0))],
        out_specs=pl.BlockSpec((tm, d), lambda i: (i, 0)),
        out_shape=jax.ShapeDtypeStruct((m, d), BF16),
        compiler_params=_cparams(("parallel",)),
        name="rmsnorm",
    )(x, g.reshape(1, d))


def _norm_residual_kernel(f_ref, x_ref, gpost_ref, *rest, has_next):
    f = f_ref[...]
    y = f * lax.rsqrt(jnp.mean(f * f, axis=-1, keepdims=True) + NORM_EPS) * gpost_ref[...]
    xn = x_ref[...] + y
    if has_next:
        gnext_ref, xo_ref, ho_ref = rest
        xo_ref[...] = xn
        h = xn * lax.rsqrt(jnp.mean(xn * xn, axis=-1, keepdims=True) + NORM_EPS) * gnext_ref[...]
        ho_ref[...] = h.astype(ho_ref.dtype)
    else:
        (xo_ref,) = rest
        xo_ref[...] = xn


def _norm_residual(f, x, g_post, g_next):
    m, d = x.shape
    tm = _tile(m, 256)
    row = pl.BlockSpec((tm, d), lambda i: (i, 0))
    vec = pl.BlockSpec((1, d), lambda i: (0, 0))
    has_next = g_next is not None
    ins = [f, x, g_post.reshape(1, d)] + ([g_next.reshape(1, d)] if has_next else [])
    out = pl.pallas_call(
        functools.partial(_norm_residual_kernel, has_next=has_next),
        grid=(m // tm,),
        in_specs=[row, row, vec] + ([vec] if has_next else []),
        out_specs=[row, row] if has_next else [row],
        out_shape=([jax.ShapeDtypeStruct((m, d), F32), jax.ShapeDtypeStruct((m, d), BF16)]
                   if has_next else [jax.ShapeDtypeStruct((m, d), F32)]),
        compiler_params=_cparams(("parallel",)),
        name="norm_residual",
    )(*ins)
    return (out[0], out[1]) if has_next else (out[0], None)


def _proj_kernel(*refs, rope, scale, tn):
    if rope:
        a_ref, w_ref, cos_ref, sin_ref, *outs = refs
    else:
        a_ref, w_ref, *outs = refs
    acc = jnp.dot(a_ref[...], w_ref[...], preferred_element_type=F32)
    if not rope:
        for o in outs:
            o[...] = acc.astype(o.dtype)
        return
    cos = cos_ref[...]
    sin = sin_ref[...]
    for g in range(tn // LANES):
        x = acc[:, g * LANES:(g + 1) * LANES]
        y = x * cos + pltpu.roll(x, LANES // 2, 1) * sin
        if scale != 1.0:
            y = y * scale
        for o in outs:
            o[:, g * LANES:(g + 1) * LANES] = y.astype(o.dtype)


def _proj(a, w, out_dtypes, cos=None, sin=None, scale=1.0):
    m, k = a.shape
    n = w.shape[1]
    tm, tn = _tile(m, MM_TILE), _tile(n, MM_TILE)
    rope = cos is not None
    in_specs = [pl.BlockSpec((tm, k), lambda i, j: (i, 0)), pl.BlockSpec((k, tn), lambda i, j: (0, j))]
    ins = [a, w]
    if rope:
        in_specs += [pl.BlockSpec((tm, LANES), lambda i, j: (i, 0))] * 2
        ins += [cos, sin]
    out = pl.pallas_call(
        functools.partial(_proj_kernel, rope=rope, scale=scale, tn=tn),
        grid=(m // tm, n // tn),
        in_specs=in_specs,
        out_specs=[pl.BlockSpec((tm, tn), lambda i, j: (i, j)) for _ in out_dtypes],
        out_shape=[jax.ShapeDtypeStruct((m, n), dt) for dt in out_dtypes],
        compiler_params=_cparams(("parallel", "parallel")),
        name="proj_rope" if rope else "proj",
    )(*ins)
    return out


def _mm2_kernel(a1_ref, a2_ref, w1_ref, w2_ref, o_ref):
    o_ref[...] = (jnp.dot(a1_ref[...], w1_ref[...], preferred_element_type=F32)
                  + jnp.dot(a2_ref[...], w2_ref[...], preferred_element_type=F32))


def _mm2(a1, a2, w1, w2):
    m, k1 = a1.shape
    k2 = a2.shape[1]
    n = w1.shape[1]
    tm, tn = _tile(m, MM_TILE), _tile(n, MM_TILE)
    return pl.pallas_call(
        _mm2_kernel,
        grid=(m // tm, n // tn),
        in_specs=[pl.BlockSpec((tm, k1), lambda i, j: (i, 0)), pl.BlockSpec((tm, k2), lambda i, j: (i, 0)),
                  pl.BlockSpec((k1, tn), lambda i, j: (0, j)), pl.BlockSpec((k2, tn), lambda i, j: (0, j))],
        out_specs=pl.BlockSpec((tm, tn), lambda i, j: (i, j)),
        out_shape=jax.ShapeDtypeStruct((m, n), F32),
        compiler_params=_cparams(("parallel", "parallel")),
        name="out_proj",
    )(a1, a2, w1, w2)


def _ffn_up_kernel(a_ref, w_ref, o_ref):
    u = jnp.maximum(jnp.dot(a_ref[...], w_ref[...], preferred_element_type=F32), 0.0)
    o_ref[...] = (u * u).astype(o_ref.dtype)


def _ffn_up(a, w):
    m, k = a.shape
    n = w.shape[1]
    tm, tn = _tile(m, MM_TILE), _tile(n, MM_TILE)
    return pl.pallas_call(
        _ffn_up_kernel,
        grid=(m // tm, n // tn),
        in_specs=[pl.BlockSpec((tm, k), lambda i, j: (i, 0)), pl.BlockSpec((k, tn), lambda i, j: (0, j))],
        out_specs=pl.BlockSpec((tm, tn), lambda i, j: (i, j)),
        out_shape=jax.ShapeDtypeStruct((m, n), BF16),
        compiler_params=_cparams(("parallel", "parallel")),
        name="ffn_up",
    )(a, w)


def _mm_acc_kernel(a_ref, w_ref, o_ref):
    part = jnp.dot(a_ref[...], w_ref[...], preferred_element_type=F32)

    @pl.when(pl.program_id(2) == 0)
    def _():
        o_ref[...] = part

    @pl.when(pl.program_id(2) != 0)
    def _():
        o_ref[...] += part


def _mm_acc(a, w, tk):
    m, k = a.shape
    n = w.shape[1]
    tm, tn, tk = _tile(m, MM_TILE), _tile(n, MM_TILE), _tile(k, tk)
    return pl.pallas_call(
        _mm_acc_kernel,
        grid=(m // tm, n // tn, k // tk),
        in_specs=[pl.BlockSpec((tm, tk), lambda i, j, kk: (i, kk)), pl.BlockSpec((tk, tn), lambda i, j, kk: (kk, j))],
        out_specs=pl.BlockSpec((tm, tn), lambda i, j, kk: (i, j)),
        out_shape=jax.ShapeDtypeStruct((m, n), F32),
        compiler_params=_cparams(("parallel", "parallel", "arbitrary")),
        name="ffn_down",
    )(a, w)


def _ple_kernel(h_ref, wg_ref, pe_ref, wp_ref, o_ref):
    gate = jnp.dot(h_ref[...], wg_ref[...], preferred_element_type=F32)
    gate = 1.0 / (1.0 + jnp.exp(-gate))
    proj = jnp.dot(pe_ref[...].astype(BF16), wp_ref[...], preferred_element_type=F32)
    o_ref[...] = gate * proj


def _ple(h, wg, pe, wp):
    m, k = h.shape
    n = wg.shape[1]
    kp = pe.shape[1]
    tm, tn = _tile(m, MM_TILE), _tile(n, MM_TILE)
    return pl.pallas_call(
        _ple_kernel,
        grid=(m // tm, n // tn),
        in_specs=[pl.BlockSpec((tm, k), lambda i, j: (i, 0)), pl.BlockSpec((k, tn), lambda i, j: (0, j)),
                  pl.BlockSpec((tm, kp), lambda i, j: (i, 0)), pl.BlockSpec((kp, tn), lambda i, j: (0, j))],
        out_specs=pl.BlockSpec((tm, tn), lambda i, j: (i, j)),
        out_shape=jax.ShapeDtypeStruct((m, n), F32),
        compiler_params=_cparams(("parallel", "parallel")),
        name="ple",
    )(h, wg, pe, wp)


def _lambda(lq1_ref, lk1_ref, lq2_ref, lk2_ref, lam_init):
    s1 = jnp.sum(lq1_ref[...] * lk1_ref[...], axis=-1, keepdims=True)
    s2 = jnp.sum(lq2_ref[...] * lk2_ref[...], axis=-1, keepdims=True)
    return jnp.exp(s1) - jnp.exp(s2) + lam_init


def _subln(o, g, lam_init):
    y = o * lax.rsqrt(jnp.mean(o * o, axis=-1, keepdims=True) + SUBLN_EPS)
    return y * g * (1.0 - lam_init)


def _prompt_attn_kernel(it_ref, jt_ref, q_ref, k_ref, v_ref, lq1_ref, lk1_ref, lq2_ref, lk2_ref, g_ref,
                        o_ref, m_sc, l_sc, acc_sc, *, lam_init, dk):
    s = pl.program_id(1)
    i = it_ref[s]
    j = jt_ref[s]

    @pl.when(j == 0)
    def _():
        m_sc[...] = jnp.full(m_sc.shape, NEG_INF, F32)
        l_sc[...] = jnp.zeros(l_sc.shape, F32)
        acc_sc[...] = jnp.zeros(acc_sc.shape, F32)

    def step(masked):
        v = v_ref[...]
        for mp in range(2):
            q = q_ref[:, mp * dk:(mp + 1) * dk]
            k = k_ref[:, mp * dk:(mp + 1) * dk]
            sc = _dot_nt(q, k)
            if masked:
                sc = jnp.where(_iota(sc.shape, 1) <= _iota(sc.shape, 0), sc, NEG_INF)
            m_prev = m_sc[mp]
            m_new = jnp.maximum(m_prev, jnp.max(sc, axis=-1, keepdims=True))
            alpha = jnp.exp(m_prev - m_new)
            p = jnp.exp(sc - m_new)
            l_sc[mp] = alpha * l_sc[mp] + jnp.sum(p, axis=-1, keepdims=True)
            acc_sc[mp] = alpha * acc_sc[mp] + _dot(p, v)
            m_sc[mp] = m_new

    @pl.when(j < i)
    def _():
        step(False)

    @pl.when(j == i)
    def _():
        step(True)
        lam = _lambda(lq1_ref, lk1_ref, lq2_ref, lk2_ref, lam_init)
        o = acc_sc[0] / l_sc[0] - lam * (acc_sc[1] / l_sc[1])
        o_ref[...] = _subln(o, g_ref[...], lam_init).astype(o_ref.dtype)


def _prompt_attention(q, k, v, lam_vecs, g, lam_init, heads, dk, dv):
    s_len = q.shape[0]
    t = _tile(s_len, ATTN_TILE)
    nb = s_len // t
    pairs = [(i, j) for i in range(nb) for j in range(i + 1)]
    it = jnp.asarray([p[0] for p in pairs], jnp.int32)
    jt = jnp.asarray([p[1] for p in pairs], jnp.int32)
    vec = pl.BlockSpec((1, dk), lambda h, s, it, jt: (0, 0))
    grid_spec = pltpu.PrefetchScalarGridSpec(
        num_scalar_prefetch=2,
        grid=(heads, len(pairs)),
        in_specs=[pl.BlockSpec((t, 2 * dk), lambda h, s, it, jt: (it[s], h)),
                  pl.BlockSpec((t, 2 * dk), lambda h, s, it, jt: (jt[s], h)),
                  pl.BlockSpec((t, dv), lambda h, s, it, jt: (jt[s], h)),
                  vec, vec, vec, vec,
                  pl.BlockSpec((1, dv), lambda h, s, it, jt: (0, 0))],
        out_specs=pl.BlockSpec((t, dv), lambda h, s, it, jt: (it[s], h)),
        scratch_shapes=[pltpu.VMEM((2, t, 1), F32), pltpu.VMEM((2, t, 1), F32), pltpu.VMEM((2, t, dv), F32)],
    )
    return pl.pallas_call(
        functools.partial(_prompt_attn_kernel, lam_init=lam_init, dk=dk),
        grid_spec=grid_spec,
        out_shape=jax.ShapeDtypeStruct((s_len, heads * dv), BF16),
        compiler_params=_cparams(("parallel", "arbitrary")),
        name="prompt_attention",
    )(it, jt, q, k, v, *lam_vecs, g.reshape(1, dv))


def _sample_attn_kernel(pt_ref, q_ref, kn_ref, vn_ref, *rest, lam_init, heads, dk, dv, n_new, pages):
    k_refs = rest[:pages]
    v_refs = rest[pages:2 * pages]
    lq1_ref, lk1_ref, lq2_ref, lk2_ref, g_ref, o_ref, qbd_sc, m_sc, l_sc, acc_sc = rest[2 * pages:]
    step_id = pl.program_id(1)
    maps = 2 * heads
    rows = n_new * maps
    width = maps * dk

    @pl.when(step_id == 0)
    def _():
        q = q_ref[...]
        qb = jnp.concatenate([jnp.broadcast_to(q[t:t + 1, :], (maps, width)) for t in range(n_new)], axis=0)
        keep = (_iota((rows, width), 1) // dk) == (_iota((rows, width), 0) % maps)
        qbd_sc[...] = jnp.where(keep, qb, 0.0).astype(BF16)
        m_sc[...] = jnp.full(m_sc.shape, NEG_INF, F32)
        l_sc[...] = jnp.zeros(l_sc.shape, F32)
        acc_sc[...] = jnp.zeros(acc_sc.shape, F32)

    def update(sc, vals):
        m_prev = m_sc[...]
        m_new = jnp.maximum(m_prev, jnp.max(sc, axis=-1, keepdims=True))
        alpha = jnp.exp(m_prev - m_new)
        p = jnp.exp(sc - m_new)
        l_sc[...] = alpha * l_sc[...] + jnp.sum(p, axis=-1, keepdims=True)
        acc_sc[...] = alpha * acc_sc[...] + _dot(p, vals)
        m_sc[...] = m_new

    for pg in range(pages):
        update(_dot_nt(qbd_sc[...], k_refs[pg][...]), v_refs[pg][...])

    @pl.when(step_id == pl.num_programs(1) - 1)
    def _():
        kn = kn_ref[...]
        sc = _dot_nt(qbd_sc[...], kn)
        sc = jnp.where(_iota(sc.shape, 1) <= _iota(sc.shape, 0) // maps, sc, NEG_INF)
        update(sc, vn_ref[...])
        lam = _lambda(lq1_ref, lk1_ref, lq2_ref, lk2_ref, lam_init)
        accn = acc_sc[...] / l_sc[...]
        own = (_iota(accn.shape, 1) // dv) == ((_iota(accn.shape, 0) % maps) // 2)
        accn = jnp.where(own, accn, 0.0)
        pad_t = o_ref.shape[0]
        r_t = _iota((pad_t, rows), 1) // maps
        r_m = _iota((pad_t, rows), 1) % maps
        hit = r_t == _iota((pad_t, rows), 0)
        sel_even = jnp.where(hit & (r_m % 2 == 0), 1.0, 0.0).astype(BF16)
        sel_odd = jnp.where(hit & (r_m % 2 == 1), 1.0, 0.0).astype(BF16)
        o = _dot_sel(sel_even, accn) - lam * _dot_sel(sel_odd, accn)
        g = g_ref[...]
        for h in range(heads):
            o_ref[:, h * dv:(h + 1) * dv] = _subln(o[:, h * dv:(h + 1) * dv], g, lam_init).astype(o_ref.dtype)


def _sample_attention(q, k_new, v_new, cache_k, cache_v, page_table, lam_vecs, g, lam_init, heads, dk, dv, n_new):
    b, pad_t, width = q.shape
    n_pages = page_table.shape[1]
    page = cache_k.shape[1]
    pages = math.gcd(n_pages, PAGES_PER_STEP)
    maps = 2 * heads
    rows = n_new * maps
    vec = pl.BlockSpec((1, dk), lambda bi, s, pt: (0, 0))
    new_spec = pl.BlockSpec((None, pad_t, width), lambda bi, s, pt: (bi, 0, 0))

    def page_spec(pg):
        return pl.BlockSpec((None, page, width), lambda bi, s, pt: (pt[bi, s * pages + pg], 0, 0))

    grid_spec = pltpu.PrefetchScalarGridSpec(
        num_scalar_prefetch=1,
        grid=(b, n_pages // pages),
        in_specs=([new_spec, new_spec, new_spec] + [page_spec(pg) for pg in range(pages)] * 2
                  + [vec, vec, vec, vec, pl.BlockSpec((1, dv), lambda bi, s, pt: (0, 0))]),
        out_specs=pl.BlockSpec((None, pad_t, heads * dv), lambda bi, s, pt: (bi, 0, 0)),
        scratch_shapes=[pltpu.VMEM((rows, width), BF16), pltpu.VMEM((rows, 1), F32), pltpu.VMEM((rows, 1), F32),
                        pltpu.VMEM((rows, heads * dv), F32)],
    )
    return pl.pallas_call(
        functools.partial(_sample_attn_kernel, lam_init=lam_init, heads=heads, dk=dk, dv=dv, n_new=n_new,
                          pages=pages),
        grid_spec=grid_spec,
        out_shape=jax.ShapeDtypeStruct((b, pad_t, heads * dv), BF16),
        compiler_params=_cparams(("parallel", "arbitrary")),
        name="sample_attention",
    )(page_table, q, k_new, v_new, *([cache_k] * pages), *([cache_v] * pages), *lam_vecs, g.reshape(1, dv))


def _segsum(x, ones_bd):
    hi = x.astype(BF16)
    lo = (x - hi.astype(F32)).astype(BF16)
    return (jnp.dot(hi, ones_bd, preferred_element_type=F32) + jnp.dot(lo, ones_bd, preferred_element_type=F32))


def _block_ones(w):
    return jnp.where(_iota((w, w), 0) // RW_N == _iota((w, w), 1) // RW_N, 1.0, 0.0).astype(BF16)


def _rw_prep_kernel(z_ref, zprev_ref, mu_ref, w0_ref, w2_ref, a0_ref, a2_ref, g2_ref, kk_ref, ka_ref,
                    r_out, lw_out, k_out, v_out, kk_out, b_out, g_out, carry_sc, *, mode, period, rw, dd, da, gp):
    z = z_ref[...]
    tm = z.shape[0]
    rolled = pltpu.roll(z, 1, 0)
    row = _iota(z.shape, 0)
    if mode == "carry":
        @pl.when(pl.program_id(0) == 0)
        def _():
            carry_sc[...] = zprev_ref[...]
        zs = jnp.where(row == 0, carry_sc[...], rolled)
        carry_sc[...] = z[tm - 1:tm, :]
    else:
        zs = jnp.where(row % period == 0, zprev_ref[...], rolled)
    zz = z + (zs - z) * mu_ref[...]
    r = zz[:, :rw]
    kr = zz[:, rw:2 * rw]
    v = zz[:, 2 * rw:3 * rw]
    zw = zz[:, 3 * rw:3 * rw + dd]
    za = zz[:, 3 * rw + dd:3 * rw + dd + da]
    zg = zz[:, 3 * rw + dd + da:3 * rw + dd + da + gp]
    u = w0_ref[...] + _dot(jnp.tanh(zw), w2_ref[...])
    w_log = jnp.minimum(u, 0.0) - jnp.log(1.0 + jnp.exp(-jnp.abs(u))) - 0.5
    lw_out[...] = -jnp.exp(w_log)
    a = 1.0 / (1.0 + jnp.exp(-(a0_ref[...] + _dot(za, a2_ref[...]))))
    g_out[...] = _dot(1.0 / (1.0 + jnp.exp(-zg)), g2_ref[...])
    ones_bd = _block_ones(LANES)
    kk = kr * kk_ref[...]
    for c in range(rw // LANES):
        sl = slice(c * LANES, (c + 1) * LANES)
        kc = kk[:, sl]
        kc = kc * lax.rsqrt(jnp.maximum(_segsum(kc * kc, ones_bd), 1e-24))
        kk_out[:, sl] = kc
        b_out[:, sl] = kc * a[:, sl]
    r_out[...] = r
    v_out[...] = v
    k_out[...] = kr * (1.0 + (a - 1.0) * ka_ref[...])


def _rw_prep(z, zprev, prm, mode, period):
    m, zp = z.shape
    rw, dd, da, gp = prm["rw"], prm["dd"], prm["da"], prm["gp"]
    tm = _tile(m, 128)
    row = pl.BlockSpec((tm, zp), lambda i: (i, 0))
    full = lambda a: pl.BlockSpec(a.shape, lambda i: (0,) * a.ndim)
    zprev_spec = full(zprev) if mode == "carry" else row
    params = [prm["mu"], prm["w0"], prm["w2"], prm["a0"], prm["a2"], prm["g2"], prm["k_k"], prm["k_a"]]
    out_spec = pl.BlockSpec((tm, rw), lambda i: (i, 0))
    return pl.pallas_call(
        functools.partial(_rw_prep_kernel, mode=mode, period=period, rw=rw, dd=dd, da=da, gp=gp),
        grid=(m // tm,),
        in_specs=[row, zprev_spec] + [full(p) for p in params],
        out_specs=[out_spec] * 7,
        out_shape=[jax.ShapeDtypeStruct((m, rw), F32)] * 7,
        scratch_shapes=[pltpu.VMEM((1, zp), F32)],
        compiler_params=_cparams(("arbitrary",)),
        name="rwkv_prep",
    )(z, zprev, *params)


def _bd(x, gw):
    head = (_iota(x.shape, 1) % gw) // RW_N
    return jnp.concatenate([jnp.where(head == h, x, 0.0) for h in range(gw // RW_N)], axis=0)


def _rw_scan_kernel(*refs, has_s0, t_valid, gw, n_groups):
    if has_s0:
        (r_ref, lw_ref, k_ref, v_ref, kk_ref, b_ref, g_ref, rk_ref, lng_ref, lnb_ref, s0_ref,
         y_out, s_out, sb_sc) = refs
    else:
        (r_ref, lw_ref, k_ref, v_ref, kk_ref, b_ref, g_ref, rk_ref, lng_ref, lnb_ref,
         y_out, s_out, sb_sc) = refs
    c = RW_CHUNK
    n = RW_N
    tc = r_ref.shape[0]
    ci = pl.program_id(1)
    last = pl.num_programs(1) - 1

    tri_inc = jnp.where(_iota((c, c), 1) <= _iota((c, c), 0), 1.0, 0.0).astype(BF16)
    s_idx = _iota((c, gw), 1) % n
    t_idx = _iota((c, gw), 0)
    strict = s_idx < t_idx
    incl = s_idx <= t_idx
    eye_w = jnp.where(s_idx == t_idx, 1.0, 0.0)
    gi0, gi1 = _iota((gw, gw), 0), _iota((gw, gw), 1)
    blk = gi0 // n == gi1 // n
    eye_g = jnp.where(gi0 == gi1, 1.0, 0.0)
    ones_bd = jnp.where(blk, 1.0, 0.0).astype(BF16)
    sel = jnp.where(_iota((gw, n), 0) % n == _iota((gw, n), 1), 1.0, 0.0).astype(BF16)
    sel_t = jnp.where(_iota((n, gw), 1) % n == _iota((n, gw), 0), 1.0, 0.0).astype(BF16)
    valid = _iota((tc, gw), 0) < t_valid

    def load(ref, sl, fill=0.0):
        x = ref[:, sl]
        if t_valid < tc:
            x = jnp.where(valid, x, fill)
        if tc < c:
            x = jnp.concatenate([x, jnp.full((c - tc, gw), fill, F32)], axis=0)
        return x

    for gi in range(n_groups):
        sl = slice(gi * gw, (gi + 1) * gw)

        @pl.when(ci == 0)
        def _():
            if has_s0:
                sb_sc[gi] = jnp.where(blk, _dot_sel_r(s0_ref[sl, :], sel_t), 0.0)
            else:
                sb_sc[gi] = jnp.zeros((gw, gw), F32)

        r, lw, k, v, kk, b = (load(ref, sl) for ref in (r_ref, lw_ref, k_ref, v_ref, kk_ref, b_ref))
        cl = _dot_sel(tri_inc, lw)
        cend = cl[c - 1:c, :]
        kt = kk * jnp.exp(cl - lw)
        rt = r * jnp.exp(cl)
        g_inv = jnp.exp(-cl)
        g_rat = jnp.exp(cend - cl)
        ktil, btil = k * g_inv, b * g_inv
        khat, bhat = k * g_rat, b * g_rat

        la = _dot_nt(jnp.concatenate([kt, rt], axis=0),
                     jnp.concatenate([_bd(btil, gw), _bd(ktil, gw)], axis=0))
        lb = jnp.where(strict, la[:c, :gw], 0.0)
        lk = jnp.where(strict, la[:c, gw:], 0.0)
        ab = jnp.where(incl, la[c:, :gw], 0.0)
        ak = jnp.where(incl, la[c:, gw:], 0.0)

        x = eye_w - lb
        p = lb
        for _ in range(int(math.log2(c)) - 1):
            p = _dot(p, _bd(p, gw))
            x = x + _dot(x, _bd(p, gw))

        bdv = _bd(v, gw)
        lkv = _dot(lk, bdv)
        tw = _dot(x, _bd(jnp.concatenate([kt, lkv], axis=1), gw))
        abw = _dot(ab, _bd(tw, gw))
        rr = rt - abw[:, :gw]
        y0 = _dot(ak, bdv) - abw[:, gw:]
        tn = _dot_tn(tw, bhat)
        pt = eye_g * jnp.exp(cend) - jnp.where(blk, tn[:gw], 0.0)
        qt = jnp.where(blk, _dot_tn(v, khat) - tn[gw:], 0.0)
        s_old = sb_sc[gi]
        y = _dot_nt(rr, s_old) + y0
        s_new = _dot(s_old, pt) + qt
        sb_sc[gi] = s_new

        @pl.when(ci == last)
        def _():
            s_out[sl, :] = _dot_sel_r(s_new, sel)

        mean = _segsum(y, ones_bd) * (1.0 / n)
        d = y - mean
        var = _segsum(d * d, ones_bd) * (1.0 / n)
        yn = d * lax.rsqrt(var + RW_LN_EPS) * lng_ref[:, sl] + lnb_ref[:, sl]
        bonus = _segsum(r * k * rk_ref[:, sl], ones_bd) * v
        out = (yn + bonus)[:tc] * g_ref[:, sl]
        y_out[:, sl] = out.astype(y_out.dtype)


def _rw_scan(r, lw, k, v, kk, b, g, prm, s0, n_seq, tc, t_valid):
    m, rw = r.shape
    n_chunks = m // (n_seq * tc)
    gw = math.gcd(rw, RW_GROUP)
    n_groups = rw // gw
    row = pl.BlockSpec((tc, rw), lambda s, ci: (s * n_chunks + ci, 0))
    vec = pl.BlockSpec((1, rw), lambda s, ci: (0, 0))
    state = pl.BlockSpec((None, rw, RW_N), lambda s, ci: (s, 0, 0))
    has_s0 = s0 is not None
    ins = [r, lw, k, v, kk, b, g, prm["r_k"], prm["ln_g"], prm["ln_b"]] + ([s0] if has_s0 else [])
    return pl.pallas_call(
        functools.partial(_rw_scan_kernel, has_s0=has_s0, t_valid=t_valid, gw=gw, n_groups=n_groups),
        grid=(n_seq, n_chunks),
        in_specs=[row] * 7 + [vec] * 3 + ([state] if has_s0 else []),
        out_specs=[row, state],
        out_shape=[jax.ShapeDtypeStruct((m, rw), BF16), jax.ShapeDtypeStruct((n_seq, rw, RW_N), F32)],
        scratch_shapes=[pltpu.VMEM((n_groups, gw, gw), F32)],
        compiler_params=_cparams(("parallel", "arbitrary")),
        name="rwkv_scan",
    )(*ins)


def _rope_tables(pos, dk):
    half = dk // 2
    inv = ROPE_THETA ** (-jnp.arange(half, dtype=F32) / half)
    ang = pos.astype(F32)[:, None] * inv[None, :]
    cos, sin = jnp.cos(ang), jnp.sin(ang)
    return jnp.concatenate([cos, cos], axis=-1), jnp.concatenate([-sin, sin], axis=-1)


def _prep_weights(i, dims, w_in, rw_mu, rw_w0, rw_w2, rw_a0, rw_a2, rw_g2, rw_k_k, rw_k_a, rw_r_k, rw_ln_g,
                  rw_ln_b, w_out, w_up, w_down, w_ple_gate, w_ple_proj):
    qw, da_w, rw, dd, da, dg, gp, zp = (dims[n] for n in ("qw", "da_w", "rw", "dd", "da", "dg", "gp", "zp"))
    rw_cols = 3 * rw + dd + da + dg
    win = w_in[i]
    w = {
        "q": win[:, :qw].astype(BF16),
        "k": win[:, qw:2 * qw].astype(BF16),
        "v": win[:, 2 * qw:2 * qw + da_w].astype(BF16),
        "z": jnp.pad(win[:, 2 * qw + da_w:], ((0, 0), (0, zp - rw_cols))).astype(BF16),
        "out_a": w_out[i][:da_w].astype(BF16),
        "out_r": w_out[i][da_w:].astype(BF16),
        "up": w_up[i].astype(BF16),
        "down": w_down[i].astype(BF16),
        "ple_gate": w_ple_gate[i].astype(BF16),
        "ple_proj": w_ple_proj[i].astype(BF16),
    }
    prm = {
        "rw": rw, "dd": dd, "da": da, "gp": gp,
        "mu": jnp.pad(rw_mu[i], (0, zp - rw_cols)).reshape(1, zp),
        "w0": rw_w0[i].reshape(1, rw), "w2": rw_w2[i].astype(BF16),
        "a0": rw_a0[i].reshape(1, rw), "a2": rw_a2[i].astype(BF16),
        "g2": jnp.pad(rw_g2[i], ((0, gp - dg), (0, 0))).astype(BF16),
        "k_k": rw_k_k[i].reshape(1, rw), "k_a": rw_k_a[i].reshape(1, rw),
        "r_k": rw_r_k[i].reshape(1, rw), "ln_g": rw_ln_g[i].reshape(1, rw), "ln_b": rw_ln_b[i].reshape(1, rw),
    }
    return w, prm


def _tail(x, o, rwo, pe, w, gains):
    g_post_mix, g_pre_ffn, g_post_ffn, g_pre_ple, g_post_ple = gains
    mix = _mm2(o, rwo, w["out_a"], w["out_r"])
    x, h = _norm_residual(mix, x, g_post_mix, g_pre_ffn)
    f = _mm_acc(_ffn_up(h, w["up"]), w["down"], tk=4096)
    x, h = _norm_residual(f, x, g_post_ffn, g_pre_ple)
    x, _ = _norm_residual(_ple(h, w["ple_gate"], pe, w["ple_proj"]), x, g_post_ple, None)
    return x


def kernel(x_prompt, x_sample, cache_k, cache_v, state_wkv, state_shift, page_table, p_prompt, p_sample, g_pre_mix, w_in, lam_q1, lam_k1, lam_q2, lam_k2, da_subln_g, rw_mu, rw_w0, rw_w2, rw_a0, rw_a2, rw_g2, rw_k_k, rw_k_a, rw_r_k, rw_ln_g, rw_ln_b, w_out, g_post_mix, g_pre_ffn, w_up, w_down, g_post_ffn, g_pre_ple, w_ple_gate, w_ple_proj, g_post_ple):
    batch, seq, d = x_prompt.shape
    db, dseq, _ = x_sample.shape
    depth, n_pool, page, maps, dk = cache_k.shape
    heads, dv = cache_v.shape[3], cache_v.shape[4]
    rwh, rwn = rw_r_k.shape[1], rw_r_k.shape[2]
    assert maps == 2 * heads and dk == LANES and rwn == RW_N and seq % RW_CHUNK == 0
    qw, da_w, rw = maps * dk, heads * dv, rwh * rwn
    dd, da, dg = rw_w2.shape[1], rw_a2.shape[1], rw_g2.shape[1]
    assert dd % LANES == 0 and da % LANES == 0 and rw % LANES == 0
    gp = _round_up(dg, LANES)
    rw_cols = 3 * rw + dd + da + dg
    zp = _round_up(3 * rw + dd + da + gp, min(MM_TILE, _round_up(rw_cols, LANES)))
    dims = dict(qw=qw, da_w=da_w, rw=rw, dd=dd, da=da, dg=dg, gp=gp, zp=zp)
    n_pages = page_table.shape[1]
    past_len = n_pages * page
    pad_t = _round_up(dseq, SUBLANES)
    pad_a = _round_up(dseq, 2 * SUBLANES)
    scale = dk ** -0.5

    cos_p, sin_p = _rope_tables(jnp.tile(jnp.arange(seq, dtype=jnp.int32), batch), dk)
    cos_s, sin_s = _rope_tables(jnp.tile(past_len + jnp.arange(dseq, dtype=jnp.int32), db), dk)

    xp = x_prompt.reshape(batch * seq, d)
    xs = x_sample.reshape(db * dseq, d)
    outs = [[] for _ in range(8)]
    for i in range(depth):
        lam_init = 0.8 - 0.6 * math.exp(-0.3 * i)
        w, prm = _prep_weights(i, dims, w_in, rw_mu, rw_w0, rw_w2, rw_a0, rw_a2, rw_g2, rw_k_k, rw_k_a, rw_r_k,
                               rw_ln_g, rw_ln_b, w_out, w_up, w_down, w_ple_gate, w_ple_proj)
        lam_vecs = [v[i].reshape(1, dk) for v in (lam_q1, lam_k1, lam_q2, lam_k2)]
        gains = (g_post_mix[i], g_pre_ffn[i], g_post_ffn[i], g_pre_ple[i], g_post_ple[i])

        h = _rmsnorm(xp, g_pre_mix[i], NORM_EPS)
        (q,) = _proj(h, w["q"], [BF16], cos_p, sin_p, scale)
        k32, k16 = _proj(h, w["k"], [F32, BF16], cos_p, sin_p)
        v32, v16 = _proj(h, w["v"], [F32, BF16])
        (z,) = _proj(h, w["z"], [F32])
        o_rows, rw_rows, wkv_rows = [], [], []
        for bi in range(batch):
            rs = slice(bi * seq, (bi + 1) * seq)
            o_rows.append(_prompt_attention(q[rs], k16[rs], v16[rs], lam_vecs, da_subln_g[i], lam_init,
                                            heads, dk, dv))
            mixer_in = _rw_prep(z[rs], jnp.zeros((1, zp), F32), prm, "carry", 0)
            rwo, s_fin = _rw_scan(*mixer_in, prm, None, 1, RW_CHUNK, RW_CHUNK)
            rw_rows.append(rwo)
            wkv_rows.append(s_fin.reshape(rwh, rwn, rwn))
        o = o_rows[0] if batch == 1 else jnp.concatenate(o_rows, axis=0)
        rwo = rw_rows[0] if batch == 1 else jnp.concatenate(rw_rows, axis=0)
        xp = _tail(xp, o, rwo, p_prompt[i].reshape(batch * seq, -1), w, gains)
        outs[0].append(k32.reshape(batch, seq, maps, dk))
        outs[1].append(v32.reshape(batch, seq, heads, dv))
        outs[4].append(jnp.stack(wkv_rows, 0))
        outs[6].append(z.reshape(batch, seq, zp)[:, -1, :rw_cols])

        h = _rmsnorm(xs, g_pre_mix[i], NORM_EPS)
        (q,) = _proj(h, w["q"], [F32], cos_s, sin_s, scale)
        (k32,) = _proj(h, w["k"], [F32], cos_s, sin_s)
        (v32,) = _proj(h, w["v"], [F32])
        (z,) = _proj(h, w["z"], [F32])
        pad3 = lambda a, t: jnp.pad(a.reshape(db, dseq, -1), ((0, 0), (0, t - dseq), (0, 0)))
        o = _sample_attention(pad3(q, pad_a), pad3(k32, pad_a), pad3(v32, pad_a), cache_k[i].reshape(n_pool, page, qw),
                              cache_v[i].reshape(n_pool, page, da_w), page_table, lam_vecs, da_subln_g[i],
                              lam_init, heads, dk, dv, dseq)
        o = o[:, :dseq].reshape(db * dseq, da_w)
        zprev = jnp.zeros((db, pad_t, zp), F32).at[:, 0, :rw_cols].set(state_shift[i])
        mixer_in = _rw_prep(pad3(z, pad_t).reshape(db * pad_t, zp), zprev.reshape(db * pad_t, zp), prm, "rows", pad_t)
        rwo, s_fin = _rw_scan(*mixer_in, prm, state_wkv[i].reshape(db, rw, rwn), db, pad_t, dseq)
        rwo = rwo.reshape(db, pad_t, rw)[:, :dseq].reshape(db * dseq, rw)
        xs = _tail(xs, o, rwo, p_sample[i].reshape(db * dseq, -1), w, gains)
        outs[2].append(k32.reshape(db, dseq, maps, dk))
        outs[3].append(v32.reshape(db, dseq, heads, dv))
        outs[5].append(s_fin.reshape(db, rwh, rwn, rwn))
        outs[7].append(z.reshape(db, dseq, zp)[:, -1, :rw_cols])

    st = [jnp.stack(o, 0) for o in outs]
    return (xp.reshape(batch, seq, d), xs.reshape(db, dseq, d), st[0], st[1], st[2], st[3], st[4], st[5],
            st[6], st[7])
```

```python
import functools
import math

import jax
import jax.numpy as jnp
from jax import lax
from jax.experimental import pallas as pl
from jax.experimental.pallas import tpu as pltpu

F32 = jnp.float32
BF16 = jnp.bfloat16

LANES = 128
SUBLANES = 8
VMEM_LIMIT_BYTES = 56 * 1024 * 1024

ROPE_THETA = 10000.0
NORM_EPS = 1e-6
SUBLN_EPS = 1e-5
RW_LN_EPS = 64e-5
NEG_INF = -1e30

RW_N = 64
RW_CHUNK = 64
RW_GROUP = 256
MM_TILE = 1024
ATTN_TILE = 1024
ATTN_STRIP = 32
PAGES_PER_STEP = 8


def _cparams(sem):
    return pltpu.CompilerParams(dimension_semantics=sem, vmem_limit_bytes=VMEM_LIMIT_BYTES)


def _round_up(x, m):
    return (x + m - 1) // m * m


def _tile(n, pref):
    t = min(n, pref)
    assert n % t == 0, (n, t)
    return t


def _dot(a, b):
    return jnp.dot(a.astype(BF16), b.astype(BF16), preferred_element_type=F32)


def _dot_nt(a, b):
    return lax.dot_general(a.astype(BF16), b.astype(BF16), (((1,), (1,)), ((), ())),
                           preferred_element_type=F32)


def _dot_tn(a, b):
    return lax.dot_general(a.astype(BF16), b.astype(BF16), (((0,), (0,)), ((), ())),
                           preferred_element_type=F32)


def _split3(x):
    hi = x.astype(BF16)
    r1 = x - hi.astype(F32)
    mid = r1.astype(BF16)
    lo = (r1 - mid.astype(F32)).astype(BF16)
    return hi, mid, lo


def _dot_sel(sel, x):
    hi, mid, lo = _split3(x)
    n = x.shape[1]
    y = jnp.dot(sel, jnp.concatenate([hi, mid, lo], axis=1), preferred_element_type=F32)
    return y[:, :n] + y[:, n:2 * n] + y[:, 2 * n:]


def _dot_sel_r(x, sel):
    hi, mid, lo = _split3(x)
    m = x.shape[0]
    y = jnp.dot(jnp.concatenate([hi, mid, lo], axis=0), sel, preferred_element_type=F32)
    return y[:m] + y[m:2 * m] + y[2 * m:]


def _iota(shape, dim):
    return lax.broadcasted_iota(jnp.int32, shape, dim)


def _rmsnorm_kernel(x_ref, g_ref, o_ref, *, eps):
    x = x_ref[...]
    y = x * lax.rsqrt(jnp.mean(x * x, axis=-1, keepdims=True) + eps)
    o_ref[...] = (y * g_ref[...]).astype(o_ref.dtype)


def _rmsnorm(x, g, eps):
    m, d = x.shape
    tm = _tile(m, 256)
    return pl.pallas_call(
        functools.partial(_rmsnorm_kernel, eps=eps),
        grid=(m // tm,),
        in_specs=[pl.BlockSpec((tm, d), lambda i: (i, 0)), pl.BlockSpec((1, d), lambda i: (0, 0))],
        out_specs=pl.BlockSpec((tm, d), lambda i: (i, 0)),
        out_shape=jax.ShapeDtypeStruct((m, d), BF16),
        compiler_params=_cparams(("parallel",)),
        name="rmsnorm",
    )(x, g.reshape(1, d))


def _norm_residual_kernel(f_ref, x_ref, gpost_ref, *rest, has_next):
    f = f_ref[...]
    y = f * lax.rsqrt(jnp.mean(f * f, axis=-1, keepdims=True) + NORM_EPS) * gpost_ref[...]
    xn = x_ref[...] + y
    if has_next:
        gnext_ref, xo_ref, ho_ref = rest
        xo_ref[...] = xn
        h = xn * lax.rsqrt(jnp.mean(xn * xn, axis=-1, keepdims=True) + NORM_EPS) * gnext_ref[...]
        ho_ref[...] = h.astype(ho_ref.dtype)
    else:
        (xo_ref,) = rest
        xo_ref[...] = xn


def _norm_residual(f, x, g_post, g_next):
    m, d = x.shape
    tm = _tile(m, 256)
    row = pl.BlockSpec((tm, d), lambda i: (i, 0))
    vec = pl.BlockSpec((1, d), lambda i: (0, 0))
    has_next = g_next is not None
    ins = [f, x, g_post.reshape(1, d)] + ([g_next.reshape(1, d)] if has_next else [])
    out = pl.pallas_call(
        functools.partial(_norm_residual_kernel, has_next=has_next),
        grid=(m // tm,),
        in_specs=[row, row, vec] + ([vec] if has_next else []),
        out_specs=[row, row] if has_next else [row],
        out_shape=([jax.ShapeDtypeStruct((m, d), F32), jax.ShapeDtypeStruct((m, d), BF16)]
                   if has_next else [jax.ShapeDtypeStruct((m, d), F32)]),
        compiler_params=_cparams(("parallel",)),
        name="norm_residual",
    )(*ins)
    return (out[0], out[1]) if has_next else (out[0], None)


def _proj_kernel(*refs, rope, scale, tn, hw):
    if rope:
        a_ref, w_ref, cos_ref, sin_ref, *outs = refs
    else:
        a_ref, w_ref, *outs = refs
    acc = jnp.dot(a_ref[...], w_ref[...], preferred_element_type=F32)
    if rope:
        cos = cos_ref[...]
        sin = sin_ref[...]
    for g in range(tn // hw):
        x = acc[:, g * hw:(g + 1) * hw]
        if rope:
            x = x * cos + pltpu.roll(x, hw // 2, 1) * sin
        if scale != 1.0:
            x = x * scale
        for o in outs:
            if len(o.shape) == 3:
                o[:, g, :] = x.astype(o.dtype)
            else:
                o[:, g * hw:(g + 1) * hw] = x.astype(o.dtype)


def _proj(a, w, outs, cos=None, sin=None, scale=1.0, hw=None, tm=MM_TILE, tn=MM_TILE):
    m, k = a.shape
    n = w.shape[1]
    tm, tn = _tile(m, tm), _tile(n, tn)
    hw = tn if hw is None else hw
    rope = cos is not None
    w_mode = dict(pipeline_mode=pl.Buffered(1)) if n == tn else {}
    in_specs = [pl.BlockSpec((tm, k), lambda i, j: (i, 0)), pl.BlockSpec((k, tn), lambda i, j: (0, j), **w_mode)]
    ins = [a, w]
    if rope:
        in_specs += [pl.BlockSpec((tm, hw), lambda i, j: (i, 0))] * 2
        ins += [cos, sin]
    out_specs, out_shape = [], []
    for dt, head_major in outs:
        if head_major:
            out_specs.append(pl.BlockSpec((tm, tn // hw, hw), lambda i, j: (i, j, 0)))
            out_shape.append(jax.ShapeDtypeStruct((m, n // hw, hw), dt))
        else:
            out_specs.append(pl.BlockSpec((tm, tn), lambda i, j: (i, j)))
            out_shape.append(jax.ShapeDtypeStruct((m, n), dt))
    return pl.pallas_call(
        functools.partial(_proj_kernel, rope=rope, scale=scale, tn=tn, hw=hw),
        grid=(m // tm, n // tn),
        in_specs=in_specs,
        out_specs=out_specs,
        out_shape=out_shape,
        compiler_params=_cparams(("parallel", "parallel")),
        name="proj_rope" if rope else "proj",
    )(*ins)


def _mm2_kernel(a1_ref, a2_ref, w1_ref, w2_ref, o_ref):
    o_ref[...] = (jnp.dot(a1_ref[...], w1_ref[...], preferred_element_type=F32)
                  + jnp.dot(a2_ref[...], w2_ref[...], preferred_element_type=F32))


def _mm2(a1, a2, w1, w2):
    m, k1 = a1.shape
    k2 = a2.shape[1]
    n = w1.shape[1]
    tm, tn = _tile(m, MM_TILE), _tile(n, MM_TILE)
    return pl.pallas_call(
        _mm2_kernel,
        grid=(m // tm, n // tn),
        in_specs=[pl.BlockSpec((tm, k1), lambda i, j: (i, 0)), pl.BlockSpec((tm, k2), lambda i, j: (i, 0)),
                  pl.BlockSpec((k1, tn), lambda i, j: (0, j)), pl.BlockSpec((k2, tn), lambda i, j: (0, j))],
        out_specs=pl.BlockSpec((tm, tn), lambda i, j: (i, j)),
        out_shape=jax.ShapeDtypeStruct((m, n), F32),
        compiler_params=_cparams(("parallel", "parallel")),
        name="out_proj",
    )(a1, a2, w1, w2)


def _ffn_up_kernel(a_ref, w_ref, o_ref):
    u = jnp.maximum(jnp.dot(a_ref[...], w_ref[...], preferred_element_type=F32), 0.0)
    o_ref[...] = (u * u).astype(o_ref.dtype)


def _ffn_up(a, w):
    m, k = a.shape
    n = w.shape[1]
    tm, tn = _tile(m, MM_TILE), _tile(n, MM_TILE)
    return pl.pallas_call(
        _ffn_up_kernel,
        grid=(m // tm, n // tn),
        in_specs=[pl.BlockSpec((tm, k), lambda i, j: (i, 0)), pl.BlockSpec((k, tn), lambda i, j: (0, j))],
        out_specs=pl.BlockSpec((tm, tn), lambda i, j: (i, j)),
        out_shape=jax.ShapeDtypeStruct((m, n), BF16),
        compiler_params=_cparams(("parallel", "parallel")),
        name="ffn_up",
    )(a, w)


def _mm_acc_kernel(a_ref, w_ref, o_ref):
    part = jnp.dot(a_ref[...], w_ref[...], preferred_element_type=F32)

    @pl.when(pl.program_id(2) == 0)
    def _():
        o_ref[...] = part

    @pl.when(pl.program_id(2) != 0)
    def _():
        o_ref[...] += part


def _mm_acc(a, w, tk):
    m, k = a.shape
    n = w.shape[1]
    tm, tn, tk = _tile(m, MM_TILE), _tile(n, MM_TILE), _tile(k, tk)
    return pl.pallas_call(
        _mm_acc_kernel,
        grid=(m // tm, n // tn, k // tk),
        in_specs=[pl.BlockSpec((tm, tk), lambda i, j, kk: (i, kk)), pl.BlockSpec((tk, tn), lambda i, j, kk: (kk, j))],
        out_specs=pl.BlockSpec((tm, tn), lambda i, j, kk: (i, j)),
        out_shape=jax.ShapeDtypeStruct((m, n), F32),
        compiler_params=_cparams(("parallel", "parallel", "arbitrary")),
        name="ffn_down",
    )(a, w)


def _ple_kernel(h_ref, wg_ref, pe_ref, wp_ref, o_ref):
    gate = jnp.dot(h_ref[...], wg_ref[...], preferred_element_type=F32)
    gate = 1.0 / (1.0 + jnp.exp(-gate))
    proj = jnp.dot(pe_ref[...].astype(BF16), wp_ref[...], preferred_element_type=F32)
    o_ref[...] = gate * proj


def _ple(h, wg, pe, wp):
    m, k = h.shape
    n = wg.shape[1]
    kp = pe.shape[1]
    tm, tn = _tile(m, MM_TILE), _tile(n, MM_TILE)
    return pl.pallas_call(
        _ple_kernel,
        grid=(m // tm, n // tn),
        in_specs=[pl.BlockSpec((tm, k), lambda i, j: (i, 0)), pl.BlockSpec((k, tn), lambda i, j: (0, j)),
                  pl.BlockSpec((tm, kp), lambda i, j: (i, 0)), pl.BlockSpec((kp, tn), lambda i, j: (0, j))],
        out_specs=pl.BlockSpec((tm, tn), lambda i, j: (i, j)),
        out_shape=jax.ShapeDtypeStruct((m, n), F32),
        compiler_params=_cparams(("parallel", "parallel")),
        name="ple",
    )(h, wg, pe, wp)


def _lambda(lq1_ref, lk1_ref, lq2_ref, lk2_ref, lam_init):
    s1 = jnp.sum(lq1_ref[...] * lk1_ref[...], axis=-1, keepdims=True)
    s2 = jnp.sum(lq2_ref[...] * lk2_ref[...], axis=-1, keepdims=True)
    return jnp.exp(s1) - jnp.exp(s2) + lam_init


def _subln(o, g, lam_init):
    y = o * lax.rsqrt(jnp.mean(o * o, axis=-1, keepdims=True) + SUBLN_EPS)
    return y * g * (1.0 - lam_init)


def _lane_tile(x, width):
    return jnp.concatenate([x] * (width // LANES), axis=1)


def _prompt_attn_kernel(it_ref, jt_ref, q_ref, k_ref, v_ref, lq1_ref, lk1_ref, lq2_ref, lk2_ref, g_ref,
                        o_ref, s_sc, p_sc, a_sc, m_sc, l_sc, acc_sc, *, lam_init, dk, strip):
    s = pl.program_id(1)
    i = it_ref[s]
    j = jt_ref[s]
    t = q_ref.shape[0]
    dv = v_ref.shape[1]

    @pl.when(j == 0)
    def _():
        m_sc[...] = jnp.full(m_sc.shape, NEG_INF, F32)
        l_sc[...] = jnp.zeros(l_sc.shape, F32)
        acc_sc[...] = jnp.zeros(acc_sc.shape, F32)

    def step(masked):
        for mp in range(2):
            s_sc[mp] = _dot_nt(q_ref[:, mp * dk:(mp + 1) * dk], k_ref[:, mp * dk:(mp + 1) * dk])

        v = v_ref[...]
        for mp in range(2):
            for r in range(t // strip):
                rows = slice(r * strip, (r + 1) * strip)
                sc = s_sc[mp, rows, :]
                if masked:
                    sc = jnp.where(_iota(sc.shape, 1) <= _iota(sc.shape, 0) + r * strip, sc, NEG_INF)
                m_prev = m_sc[mp, rows, :]
                m_new = jnp.maximum(m_prev, jnp.max(sc, axis=-1, keepdims=True))
                alpha = jnp.exp2(m_prev - m_new)
                p = jnp.exp2(sc - _lane_tile(m_new, t))
                l_sc[mp, rows, :] = alpha * l_sc[mp, rows, :] + jnp.sum(p, axis=-1, keepdims=True)
                a_sc[mp, rows, :] = alpha
                m_sc[mp, rows, :] = m_new
                p_sc[mp, rows, :] = p.astype(BF16)
            acc_sc[mp] = (_lane_tile(a_sc[mp], dv) * acc_sc[mp]
                          + jnp.dot(p_sc[mp], v, preferred_element_type=F32))

    @pl.when(j < i)
    def _():
        step(False)

    @pl.when(j == i)
    def _():
        step(True)
        lam = _lambda(lq1_ref, lk1_ref, lq2_ref, lk2_ref, lam_init)
        o = acc_sc[0] / _lane_tile(l_sc[0], dv) - lam * (acc_sc[1] / _lane_tile(l_sc[1], dv))
        o_ref[...] = _subln(o, g_ref[...], lam_init).astype(o_ref.dtype)


def _prompt_attention(q, k, v, lam_vecs, g, lam_init, heads, dk, dv):
    s_len = q.shape[0]
    t = _tile(s_len, ATTN_TILE)
    nb = s_len // t
    pairs = [(i, j) for i in range(nb) for j in range(i + 1)]
    it = jnp.asarray([p[0] for p in pairs], jnp.int32)
    jt = jnp.asarray([p[1] for p in pairs], jnp.int32)
    vec = pl.BlockSpec((1, dk), lambda h, s, it, jt: (0, 0))
    grid_spec = pltpu.PrefetchScalarGridSpec(
        num_scalar_prefetch=2,
        grid=(heads, len(pairs)),
        in_specs=[pl.BlockSpec((t, 2 * dk), lambda h, s, it, jt: (it[s], h)),
                  pl.BlockSpec((t, 2 * dk), lambda h, s, it, jt: (jt[s], h)),
                  pl.BlockSpec((t, dv), lambda h, s, it, jt: (jt[s], h)),
                  vec, vec, vec, vec,
                  pl.BlockSpec((1, dv), lambda h, s, it, jt: (0, 0))],
        out_specs=pl.BlockSpec((t, dv), lambda h, s, it, jt: (it[s], h)),
        scratch_shapes=[pltpu.VMEM((2, t, t), F32), pltpu.VMEM((2, t, t), BF16), pltpu.VMEM((2, t, LANES), F32),
                        pltpu.VMEM((2, t, LANES), F32), pltpu.VMEM((2, t, LANES), F32),
                        pltpu.VMEM((2, t, dv), F32)],
    )
    return pl.pallas_call(
        functools.partial(_prompt_attn_kernel, lam_init=lam_init, dk=dk, strip=min(ATTN_STRIP, t)),
        grid_spec=grid_spec,
        out_shape=jax.ShapeDtypeStruct((s_len, heads * dv), BF16),
        compiler_params=_cparams(("parallel", "arbitrary")),
        name="prompt_attention",
    )(it, jt, q, k, v, *lam_vecs, g.reshape(1, dv))


def _sample_attn_kernel(pt_ref, q_ref, kn_ref, vn_ref, *rest, lam_init, heads, dk, dv, pages):
    k_refs = rest[:pages]
    v_refs = rest[pages:2 * pages]
    (lq1_ref, lk1_ref, lq2_ref, lk2_ref, g_ref, o_ref, q_sc, bias_sc, m_sc, l_sc, acc_sc) = rest[2 * pages:]
    step_id = pl.program_id(1)
    half = SUBLANES // 2
    rp = heads * half
    page_lanes = bias_sc.shape[1]

    def own_head(shape):
        return _iota(shape, 1) % heads == (_iota(shape, 0) % rp) // half

    @pl.when(step_id == 0)
    def _():
        q = q_ref[0:SUBLANES, :]
        for par in range(2):
            for j in range(heads // 2):
                lo = q[:, (4 * j + par) * dk:(4 * j + par + 1) * dk]
                hi = q[:, (4 * j + 2 + par) * dk:(4 * j + 3 + par) * dk]
                q_sc[par * rp + j * SUBLANES:par * rp + (j + 1) * SUBLANES, :] = lo + pltpu.roll(hi, half, 0)
        bias_sc[...] = jnp.where(own_head(bias_sc.shape), 0.0, NEG_INF)
        m_sc[...] = jnp.full(m_sc.shape, NEG_INF, F32)
        l_sc[...] = jnp.zeros(l_sc.shape, F32)
        acc_sc[...] = jnp.zeros(acc_sc.shape, F32)

    def scores(k_ref, lanes):
        return jnp.concatenate(
            [_dot_nt(q_sc[par * rp:(par + 1) * rp, :], k_ref[pl.ds(par, lanes, stride=2), :]) for par in range(2)],
            axis=0)

    def update(sc, v_list):
        m_prev = m_sc[...]
        m_new = jnp.maximum(m_prev, jnp.max(sc, axis=-1, keepdims=True))
        alpha = jnp.exp(m_prev - m_new)
        p = jnp.exp(sc - m_new)
        l_sc[...] = alpha * l_sc[...] + jnp.sum(p, axis=-1, keepdims=True)
        pv, off = None, 0
        for vr in v_list:
            n = vr.shape[0]
            part = _dot(p[:, off:off + n], vr[...])
            pv = part if pv is None else pv + part
            off += n
        acc_sc[...] = alpha * acc_sc[...] + pv
        m_sc[...] = m_new

    bias = bias_sc[...]
    update(jnp.concatenate([scores(kr, page_lanes) + bias for kr in k_refs], axis=1), v_refs)

    @pl.when(step_id == pl.num_programs(1) - 1)
    def _():
        sc = scores(kn_ref, vn_ref.shape[0])
        ok = own_head(sc.shape) & (_iota(sc.shape, 1) // heads <= _iota(sc.shape, 0) % half)
        update(jnp.where(ok, sc, NEG_INF), [vn_ref])
        lam = _lambda(lq1_ref, lk1_ref, lq2_ref, lk2_ref, lam_init)
        accn = acc_sc[...] / l_sc[...]
        y = _subln(accn[:rp] - lam * accn[rp:], g_ref[...], lam_init)
        for j in range(heads // 2):
            y8 = y[j * SUBLANES:(j + 1) * SUBLANES]
            o_ref[:, (2 * j) * dv:(2 * j + 1) * dv] = y8
            o_ref[:, (2 * j + 1) * dv:(2 * j + 2) * dv] = pltpu.roll(y8, half, 0)


def _sample_attention(q, k_new, v_new, cache_k, cache_v, pool_offset, page_table, lam_vecs, g, lam_init, n_new):
    b, pad_t, width = q.shape
    n_pages = page_table.shape[1]
    dk, dv = cache_k.shape[2], cache_v.shape[2]
    maps = width // dk
    heads = maps // 2
    page = cache_k.shape[1] // maps
    half = SUBLANES // 2
    assert n_new <= half and pad_t >= SUBLANES and heads % 2 == 0
    pages = math.gcd(n_pages, PAGES_PER_STEP)
    rows = 2 * heads * half
    vec = pl.BlockSpec((1, dk), lambda bi, s, pt: (0, 0))
    per_row = lambda a: pl.BlockSpec((None,) + a.shape[1:], lambda bi, s, pt: (bi, 0, 0))

    def page_spec(pg, a):
        return pl.BlockSpec((None,) + a.shape[1:], lambda bi, s, pt: (pool_offset + pt[bi, s * pages + pg], 0, 0))

    grid_spec = pltpu.PrefetchScalarGridSpec(
        num_scalar_prefetch=1,
        grid=(b, n_pages // pages),
        in_specs=([per_row(q), per_row(k_new), per_row(v_new)] + [page_spec(pg, cache_k) for pg in range(pages)]
                  + [page_spec(pg, cache_v) for pg in range(pages)]
                  + [vec, vec, vec, vec, pl.BlockSpec((1, dv), lambda bi, s, pt: (0, 0))]),
        out_specs=pl.BlockSpec((None, SUBLANES, heads * dv), lambda bi, s, pt: (bi, 0, 0)),
        scratch_shapes=[pltpu.VMEM((rows, dk), F32), pltpu.VMEM((rows, page * heads), F32),
                        pltpu.VMEM((rows, 1), F32), pltpu.VMEM((rows, 1), F32), pltpu.VMEM((rows, dv), F32)],
    )
    return pl.pallas_call(
        functools.partial(_sample_attn_kernel, lam_init=lam_init, heads=heads, dk=dk, dv=dv, pages=pages),
        grid_spec=grid_spec,
        out_shape=jax.ShapeDtypeStruct((b, SUBLANES, heads * dv), F32),
        compiler_params=_cparams(("parallel", "arbitrary")),
        name="sample_attention",
    )(page_table, q, k_new, v_new, *([cache_k] * pages), *([cache_v] * pages), *lam_vecs, g.reshape(1, dv))


def _segsum(x, ones_bd):
    hi = x.astype(BF16)
    lo = (x - hi.astype(F32)).astype(BF16)
    return (jnp.dot(hi, ones_bd, preferred_element_type=F32) + jnp.dot(lo, ones_bd, preferred_element_type=F32))


def _block_ones(w):
    return jnp.where(_iota((w, w), 0) // RW_N == _iota((w, w), 1) // RW_N, 1.0, 0.0).astype(BF16)


def _rw_prep_kernel(z_ref, zprev_ref, mu_ref, w0_ref, w2_ref, a0_ref, a2_ref, g2_ref, kk_ref, ka_ref,
                    r_out, lw_out, k_out, v_out, kk_out, b_out, g_out, carry_sc, *, mode, period, rw, dd, da, gp):
    z = z_ref[...]
    tm = z.shape[0]
    rolled = pltpu.roll(z, 1, 0)
    row = _iota(z.shape, 0)
    if mode == "carry":
        @pl.when(pl.program_id(0) == 0)
        def _():
            carry_sc[...] = zprev_ref[...]
        zs = jnp.where(row == 0, carry_sc[...], rolled)
        carry_sc[...] = z[tm - 1:tm, :]
    else:
        zs = jnp.where(row % period == 0, zprev_ref[...], rolled)
    zz = z + (zs - z) * mu_ref[...]
    r = zz[:, :rw]
    kr = zz[:, rw:2 * rw]
    v = zz[:, 2 * rw:3 * rw]
    zw = zz[:, 3 * rw:3 * rw + dd]
    za = zz[:, 3 * rw + dd:3 * rw + dd + da]
    zg = zz[:, 3 * rw + dd + da:3 * rw + dd + da + gp]
    u = w0_ref[...] + _dot(jnp.tanh(zw), w2_ref[...])
    w_log = jnp.minimum(u, 0.0) - jnp.log(1.0 + jnp.exp(-jnp.abs(u))) - 0.5
    lw_out[...] = -jnp.exp(w_log)
    a = 1.0 / (1.0 + jnp.exp(-(a0_ref[...] + _dot(za, a2_ref[...]))))
    g_out[...] = _dot(1.0 / (1.0 + jnp.exp(-zg)), g2_ref[...])
    ones_bd = _block_ones(LANES)
    kk = kr * kk_ref[...]
    for c in range(rw // LANES):
        sl = slice(c * LANES, (c + 1) * LANES)
        kc = kk[:, sl]
        kc = kc * lax.rsqrt(jnp.maximum(_segsum(kc * kc, ones_bd), 1e-24))
        kk_out[:, sl] = kc
        b_out[:, sl] = kc * a[:, sl]
    r_out[...] = r
    v_out[...] = v
    k_out[...] = kr * (1.0 + (a - 1.0) * ka_ref[...])


def _rw_prep(z, zprev, prm, mode, period):
    m, zp = z.shape
    rw, dd, da, gp = prm["rw"], prm["dd"], prm["da"], prm["gp"]
    tm = _tile(m, 128)
    row = pl.BlockSpec((tm, zp), lambda i: (i, 0))
    full = lambda a: pl.BlockSpec(a.shape, lambda i: (0,) * a.ndim)
    zprev_spec = full(zprev) if mode == "carry" else row
    params = [prm["mu"], prm["w0"], prm["w2"], prm["a0"], prm["a2"], prm["g2"], prm["k_k"], prm["k_a"]]
    out_spec = pl.BlockSpec((tm, rw), lambda i: (i, 0))
    return pl.pallas_call(
        functools.partial(_rw_prep_kernel, mode=mode, period=period, rw=rw, dd=dd, da=da, gp=gp),
        grid=(m // tm,),
        in_specs=[row, zprev_spec] + [full(p) for p in params],
        out_specs=[out_spec] * 7,
        out_shape=[jax.ShapeDtypeStruct((m, rw), F32)] * 7,
        scratch_shapes=[pltpu.VMEM((1, zp), F32)],
        compiler_params=_cparams(("arbitrary",)),
        name="rwkv_prep",
    )(z, zprev, *params)


def _bd(x, gw):
    head = (_iota(x.shape, 1) % gw) // RW_N
    return jnp.concatenate([jnp.where(head == h, x, 0.0) for h in range(gw // RW_N)], axis=0)


def _rw_scan_kernel(*refs, has_s0, t_valid, gw, n_groups):
    if has_s0:
        (r_ref, lw_ref, k_ref, v_ref, kk_ref, b_ref, g_ref, rk_ref, lng_ref, lnb_ref, s0_ref,
         y_out, s_out, sb_sc) = refs
    else:
        (r_ref, lw_ref, k_ref, v_ref, kk_ref, b_ref, g_ref, rk_ref, lng_ref, lnb_ref,
         y_out, s_out, sb_sc) = refs
    c = RW_CHUNK
    n = RW_N
    tc = r_ref.shape[0]
    ci = pl.program_id(1)
    last = pl.num_programs(1) - 1

    tri_inc = jnp.where(_iota((c, c), 1) <= _iota((c, c), 0), 1.0, 0.0).astype(BF16)
    s_idx = _iota((c, gw), 1) % n
    t_idx = _iota((c, gw), 0)
    strict = s_idx < t_idx
    incl = s_idx <= t_idx
    eye_w = jnp.where(s_idx == t_idx, 1.0, 0.0)
    gi0, gi1 = _iota((gw, gw), 0), _iota((gw, gw), 1)
    blk = gi0 // n == gi1 // n
    eye_g = jnp.where(gi0 == gi1, 1.0, 0.0)
    ones_bd = jnp.where(blk, 1.0, 0.0).astype(BF16)
    sel = jnp.where(_iota((gw, n), 0) % n == _iota((gw, n), 1), 1.0, 0.0).astype(BF16)
    sel_t = jnp.where(_iota((n, gw), 1) % n == _iota((n, gw), 0), 1.0, 0.0).astype(BF16)
    valid = _iota((tc, gw), 0) < t_valid

    def load(ref, sl, fill=0.0):
        x = ref[:, sl]
        if t_valid < tc:
            x = jnp.where(valid, x, fill)
        if tc < c:
            x = jnp.concatenate([x, jnp.full((c - tc, gw), fill, F32)], axis=0)
        return x

    @pl.when(ci == 0)
    def _():
        for gi in range(n_groups):
            if has_s0:
                sb_sc[gi] = jnp.where(blk, _dot_sel_r(s0_ref[gi * gw:(gi + 1) * gw, :], sel_t), 0.0)
            else:
                sb_sc[gi] = jnp.zeros((gw, gw), F32)

    def group_chain(gi):
        sl = slice(gi * gw, (gi + 1) * gw)
        r, lw, k, v, kk, b = (load(ref, sl) for ref in (r_ref, lw_ref, k_ref, v_ref, kk_ref, b_ref))
        cl = _dot_sel(tri_inc, lw)
        yield
        cend = cl[c - 1:c, :]
        kt = kk * jnp.exp(cl - lw)
        rt = r * jnp.exp(cl)
        g_inv = jnp.exp(-cl)
        g_rat = jnp.exp(cend - cl)
        ktil, btil = k * g_inv, b * g_inv
        khat, bhat = k * g_rat, b * g_rat
        la = _dot_nt(jnp.concatenate([kt, rt], axis=0),
                     jnp.concatenate([_bd(btil, gw), _bd(ktil, gw)], axis=0))
        yield
        lb = jnp.where(strict, la[:c, :gw], 0.0)
        lk = jnp.where(strict, la[:c, gw:], 0.0)
        ab = jnp.where(incl, la[c:, :gw], 0.0)
        ak = jnp.where(incl, la[c:, gw:], 0.0)
        bdv = _bd(v, gw)
        lkv = _dot(lk, bdv)
        y0 = _dot(ak, bdv)
        qt = _dot_tn(v, khat)
        x = eye_w - lb
        p = lb
        for _ in range(max(0, math.ceil(math.log2(t_valid)) - 1)):
            p = _dot(p, _bd(p, gw))
            yield
            x = x + _dot(x, _bd(p, gw))
            yield
        tw = _dot(x, _bd(jnp.concatenate([kt, lkv], axis=1), gw))
        yield
        abw = _dot(ab, _bd(tw, gw))
        tn = _dot_tn(tw, bhat)
        yield
        rr = rt - abw[:, :gw]
        y0 = y0 - abw[:, gw:]
        pt = eye_g * jnp.exp(cend) - jnp.where(blk, tn[:gw], 0.0)
        qt = jnp.where(blk, qt - tn[gw:], 0.0)
        s_old = sb_sc[gi]
        y = _dot_nt(rr, s_old) + y0
        sb_sc[gi] = _dot(s_old, pt) + qt
        bonus = _segsum(r * k * rk_ref[:, sl], ones_bd) * v
        yield
        mean = _segsum(y, ones_bd) * (1.0 / n)
        yield
        d = y - mean
        var = _segsum(d * d, ones_bd) * (1.0 / n)
        yield
        yn = d * lax.rsqrt(var + RW_LN_EPS) * lng_ref[:, sl] + lnb_ref[:, sl]
        out = (yn + bonus)[:tc] * g_ref[:, sl]
        y_out[:, sl] = out.astype(y_out.dtype)

    chains = [group_chain(gi) for gi in range(n_groups)]
    while chains:
        chains = [ch for ch in chains if next(ch, "done") != "done"]

    @pl.when(ci == last)
    def _():
        for gi in range(n_groups):
            s_out[gi * gw:(gi + 1) * gw, :] = _dot_sel_r(sb_sc[gi], sel)


def _rw_scan(r, lw, k, v, kk, b, g, prm, s0, n_seq, tc, t_valid):
    m, rw = r.shape
    n_chunks = m // (n_seq * tc)
    gw = math.gcd(rw, RW_GROUP)
    n_groups = rw // gw
    row = pl.BlockSpec((tc, rw), lambda s, ci: (s * n_chunks + ci, 0))
    vec = pl.BlockSpec((1, rw), lambda s, ci: (0, 0))
    state = pl.BlockSpec((None, rw, RW_N), lambda s, ci: (s, 0, 0))
    has_s0 = s0 is not None
    ins = [r, lw, k, v, kk, b, g, prm["r_k"], prm["ln_g"], prm["ln_b"]] + ([s0] if has_s0 else [])
    return pl.pallas_call(
        functools.partial(_rw_scan_kernel, has_s0=has_s0, t_valid=t_valid, gw=gw, n_groups=n_groups),
        grid=(n_seq, n_chunks),
        in_specs=[row] * 7 + [vec] * 3 + ([state] if has_s0 else []),
        out_specs=[row, state],
        out_shape=[jax.ShapeDtypeStruct((m, rw), BF16), jax.ShapeDtypeStruct((n_seq, rw, RW_N), F32)],
        scratch_shapes=[pltpu.VMEM((n_groups, gw, gw), F32)],
        compiler_params=_cparams(("parallel", "arbitrary")),
        name="rwkv_scan",
    )(*ins)


def _rope_tables(pos, dk):
    half = dk // 2
    inv = ROPE_THETA ** (-jnp.arange(half, dtype=F32) / half)
    ang = pos.astype(F32)[:, None] * inv[None, :]
    cos, sin = jnp.cos(ang), jnp.sin(ang)
    return jnp.concatenate([cos, cos], axis=-1), jnp.concatenate([-sin, sin], axis=-1)


def _prep_weights(i, dims, w_in, rw_mu, rw_w0, rw_w2, rw_a0, rw_a2, rw_g2, rw_k_k, rw_k_a, rw_r_k, rw_ln_g,
                  rw_ln_b, w_out, w_up, w_down, w_ple_gate, w_ple_proj):
    qw, da_w, rw, dd, da, dg, gp, zp = (dims[n] for n in ("qw", "da_w", "rw", "dd", "da", "dg", "gp", "zp"))
    rw_cols = 3 * rw + dd + da + dg
    win = w_in[i]
    w = {
        "q": win[:, :qw].astype(BF16),
        "k": win[:, qw:2 * qw].astype(BF16),
        "v": win[:, 2 * qw:2 * qw + da_w].astype(BF16),
        "z": jnp.pad(win[:, 2 * qw + da_w:], ((0, 0), (0, zp - rw_cols))).astype(BF16),
        "out_a": w_out[i][:da_w].astype(BF16),
        "out_r": w_out[i][da_w:].astype(BF16),
        "up": w_up[i].astype(BF16),
        "down": w_down[i].astype(BF16),
        "ple_gate": w_ple_gate[i].astype(BF16),
        "ple_proj": w_ple_proj[i].astype(BF16),
    }
    prm = {
        "rw": rw, "dd": dd, "da": da, "gp": gp,
        "mu": jnp.pad(rw_mu[i], (0, zp - rw_cols)).reshape(1, zp),
        "w0": rw_w0[i].reshape(1, rw), "w2": rw_w2[i].astype(BF16),
        "a0": rw_a0[i].reshape(1, rw), "a2": rw_a2[i].astype(BF16),
        "g2": jnp.pad(rw_g2[i], ((0, gp - dg), (0, 0))).astype(BF16),
        "k_k": rw_k_k[i].reshape(1, rw), "k_a": rw_k_a[i].reshape(1, rw),
        "r_k": rw_r_k[i].reshape(1, rw), "ln_g": rw_ln_g[i].reshape(1, rw), "ln_b": rw_ln_b[i].reshape(1, rw),
    }
    return w, prm


def _tail(x, o, rwo, pe, w, gains):
    g_post_mix, g_pre_ffn, g_post_ffn, g_pre_ple, g_post_ple = gains
    mix = _mm2(o, rwo, w["out_a"], w["out_r"])
    x, h = _norm_residual(mix, x, g_post_mix, g_pre_ffn)
    f = _mm_acc(_ffn_up(h, w["up"]), w["down"], tk=4096)
    x, h = _norm_residual(f, x, g_post_ffn, g_pre_ple)
    x, _ = _norm_residual(_ple(h, w["ple_gate"], pe, w["ple_proj"]), x, g_post_ple, None)
    return x


def kernel(x_prompt, x_sample, cache_k, cache_v, state_wkv, state_shift, page_table, p_prompt, p_sample, g_pre_mix, w_in, lam_q1, lam_k1, lam_q2, lam_k2, da_subln_g, rw_mu, rw_w0, rw_w2, rw_a0, rw_a2, rw_g2, rw_k_k, rw_k_a, rw_r_k, rw_ln_g, rw_ln_b, w_out, g_post_mix, g_pre_ffn, w_up, w_down, g_post_ffn, g_pre_ple, w_ple_gate, w_ple_proj, g_post_ple):
    batch, seq, d = x_prompt.shape
    db, dseq, _ = x_sample.shape
    depth, n_pool, page, maps, dk = cache_k.shape
    heads, dv = cache_v.shape[3], cache_v.shape[4]
    rwh, rwn = rw_r_k.shape[1], rw_r_k.shape[2]
    assert maps == 2 * heads and dk == LANES and rwn == RW_N and seq % RW_CHUNK == 0
    qw, da_w, rw = maps * dk, heads * dv, rwh * rwn
    dd, da, dg = rw_w2.shape[1], rw_a2.shape[1], rw_g2.shape[1]
    assert dd % LANES == 0 and da % LANES == 0 and rw % LANES == 0
    gp = _round_up(dg, LANES)
    rw_cols = 3 * rw + dd + da + dg
    zp = _round_up(3 * rw + dd + da + gp, min(MM_TILE, _round_up(rw_cols, LANES)))
    dims = dict(qw=qw, da_w=da_w, rw=rw, dd=dd, da=da, dg=dg, gp=gp, zp=zp)
    n_pages = page_table.shape[1]
    past_len = n_pages * page
    pad_t = _round_up(dseq, SUBLANES)
    pad_a = _round_up(dseq, 2 * SUBLANES)
    scale = dk ** -0.5

    cos_p, sin_p = _rope_tables(jnp.tile(jnp.arange(seq, dtype=jnp.int32), batch), dk)
    cos_s, sin_s = _rope_tables(jnp.tile(past_len + jnp.arange(dseq, dtype=jnp.int32), db), dk)

    xp = x_prompt.reshape(batch * seq, d)
    xs = x_sample.reshape(db * dseq, d)
    outs = [[] for _ in range(8)]
    for i in range(depth):
        lam_init = 0.8 - 0.6 * math.exp(-0.3 * i)
        w, prm = _prep_weights(i, dims, w_in, rw_mu, rw_w0, rw_w2, rw_a0, rw_a2, rw_g2, rw_k_k, rw_k_a, rw_r_k,
                               rw_ln_g, rw_ln_b, w_out, w_up, w_down, w_ple_gate, w_ple_proj)
        lam_vecs = [v[i].reshape(1, dk) for v in (lam_q1, lam_k1, lam_q2, lam_k2)]
        gains = (g_post_mix[i], g_pre_ffn[i], g_post_ffn[i], g_pre_ple[i], g_post_ple[i])

        h = _rmsnorm(xp, g_pre_mix[i], NORM_EPS)
        (q,) = _proj(h, w["q"], [(BF16, False)], cos_p, sin_p, scale * math.log2(math.e), hw=dk)
        k32, k16 = _proj(h, w["k"], [(F32, True), (BF16, False)], cos_p, sin_p, hw=dk)
        v32, v16 = _proj(h, w["v"], [(F32, True), (BF16, False)], hw=dv, tm=MM_TILE // 2, tn=da_w)
        (z,) = _proj(h, w["z"], [(F32, False)])
        o_rows, rw_rows, wkv_rows = [], [], []
        for bi in range(batch):
            rs = slice(bi * seq, (bi + 1) * seq)
            o_rows.append(_prompt_attention(q[rs], k16[rs], v16[rs], lam_vecs, da_subln_g[i], lam_init,
                                            heads, dk, dv))
            mixer_in = _rw_prep(z[rs], jnp.zeros((1, zp), F32), prm, "carry", 0)
            rwo, s_fin = _rw_scan(*mixer_in, prm, None, 1, RW_CHUNK, RW_CHUNK)
            rw_rows.append(rwo)
            wkv_rows.append(s_fin.reshape(rwh, rwn, rwn))
        o = o_rows[0] if batch == 1 else jnp.concatenate(o_rows, axis=0)
        rwo = rw_rows[0] if batch == 1 else jnp.concatenate(rw_rows, axis=0)
        xp = _tail(xp, o, rwo, p_prompt[i].reshape(batch * seq, -1), w, gains)
        outs[0].append(k32.reshape(batch, seq, maps, dk))
        outs[1].append(v32.reshape(batch, seq, heads, dv))
        outs[4].append(jnp.stack(wkv_rows, 0))
        outs[6].append(z.reshape(batch, seq, zp)[:, -1, :rw_cols])

        h = _rmsnorm(xs, g_pre_mix[i], NORM_EPS)
        (q,) = _proj(h, w["q"], [(F32, False)], cos_s, sin_s, scale, hw=dk)
        (k32,) = _proj(h, w["k"], [(F32, False)], cos_s, sin_s, hw=dk)
        (v32,) = _proj(h, w["v"], [(F32, False)])
        (z,) = _proj(h, w["z"], [(F32, False)])
        pad3 = lambda a, t: jnp.pad(a.reshape(db, dseq, -1), ((0, 0), (0, t - dseq), (0, 0)))
        o = _sample_attention(pad3(q, pad_a), pad3(k32, pad_a).reshape(db, pad_a * maps, dk),
                              pad3(v32, pad_a).reshape(db, pad_a * heads, dv),
                              cache_k.reshape(depth * n_pool, page * maps, dk),
                              cache_v.reshape(depth * n_pool, page * heads, dv), i * n_pool, page_table,
                              lam_vecs, da_subln_g[i], lam_init, dseq)
        o = o[:, :dseq].reshape(db * dseq, da_w).astype(BF16)
        zprev = jnp.zeros((db, pad_t, zp), F32).at[:, 0, :rw_cols].set(state_shift[i])
        mixer_in = _rw_prep(pad3(z, pad_t).reshape(db * pad_t, zp), zprev.reshape(db * pad_t, zp), prm, "rows", pad_t)
        rwo, s_fin = _rw_scan(*mixer_in, prm, state_wkv[i].reshape(db, rw, rwn), db, pad_t, dseq)
        rwo = rwo.reshape(db, pad_t, rw)[:, :dseq].reshape(db * dseq, rw)
        xs = _tail(xs, o, rwo, p_sample[i].reshape(db * dseq, -1), w, gains)
        outs[2].append(k32.reshape(db, dseq, maps, dk))
        outs[3].append(v32.reshape(db, dseq, heads, dv))
        outs[5].append(s_fin.reshape(db, rwh, rwn, rwn))
        outs[7].append(z.reshape(db, dseq, zp)[:, -1, :rw_cols])

    st = [jnp.stack(o, 0) for o in outs]
    return (xp.reshape(batch, seq, d), xs.reshape(db, dseq, d), st[0], st[1], st[2], st[3], st[4], st[5],
            st[6], st[7])
```

```python
import functools
import math

import jax
import jax.numpy as jnp
from jax import lax
from jax.experimental import pallas as pl
from jax.experimental.pallas import tpu as pltpu

F32 = jnp.float32
BF16 = jnp.bfloat16

LANES = 128
SUBLANES = 8
VMEM_LIMIT_BYTES = 56 * 1024 * 1024

ROPE_THETA = 10000.0
NORM_EPS = 1e-6
SUBLN_EPS = 1e-5
RW_LN_EPS = 64e-5
NEG_INF = -1e30

RW_N = 64
RW_CHUNK = 64
RW_GROUP = 256
MM_TILE = 1024
DENSE_TN = 512
FFN_TK = 4096
ATTN_TILE = 1024
ATTN_STRIP = 32
PAGES_PER_STEP = 8


def _cparams(sem):
    return pltpu.CompilerParams(dimension_semantics=sem, vmem_limit_bytes=VMEM_LIMIT_BYTES)


def _round_up(x, m):
    return (x + m - 1) // m * m


def _tile(n, pref):
    t = min(n, pref)
    assert n % t == 0, (n, t)
    return t


def _dot(a, b):
    return jnp.dot(a.astype(BF16), b.astype(BF16), preferred_element_type=F32)


def _dot_nt(a, b):
    return lax.dot_general(a.astype(BF16), b.astype(BF16), (((1,), (1,)), ((), ())),
                           preferred_element_type=F32)


def _dot_tn(a, b):
    return lax.dot_general(a.astype(BF16), b.astype(BF16), (((0,), (0,)), ((), ())),
                           preferred_element_type=F32)


def _split3(x):
    hi = x.astype(BF16)
    r1 = x - hi.astype(F32)
    mid = r1.astype(BF16)
    lo = (r1 - mid.astype(F32)).astype(BF16)
    return hi, mid, lo


def _dot_sel(sel, x):
    hi, mid, lo = _split3(x)
    n = x.shape[1]
    y = jnp.dot(sel, jnp.concatenate([hi, mid, lo], axis=1), preferred_element_type=F32)
    return y[:, :n] + y[:, n:2 * n] + y[:, 2 * n:]


def _dot_sel_r(x, sel):
    hi, mid, lo = _split3(x)
    m = x.shape[0]
    y = jnp.dot(jnp.concatenate([hi, mid, lo], axis=0), sel, preferred_element_type=F32)
    return y[:m] + y[m:2 * m] + y[2 * m:]


def _iota(shape, dim):
    return lax.broadcasted_iota(jnp.int32, shape, dim)


def _rmsnorm_kernel(x_ref, g_ref, o_ref, *, eps):
    x = x_ref[...]
    y = x * lax.rsqrt(jnp.mean(x * x, axis=-1, keepdims=True) + eps)
    o_ref[...] = (y * g_ref[...]).astype(o_ref.dtype)


def _rmsnorm(x, g, eps):
    m, d = x.shape
    tm = _tile(m, 256)
    return pl.pallas_call(
        functools.partial(_rmsnorm_kernel, eps=eps),
        grid=(m // tm,),
        in_specs=[pl.BlockSpec((tm, d), lambda i: (i, 0)), pl.BlockSpec((1, d), lambda i: (0, 0))],
        out_specs=pl.BlockSpec((tm, d), lambda i: (i, 0)),
        out_shape=jax.ShapeDtypeStruct((m, d), BF16),
        compiler_params=_cparams(("parallel",)),
        name="rmsnorm",
    )(x, g.reshape(1, d))


def _norm_residual_kernel(f_ref, x_ref, gpost_ref, *rest, has_next):
    f = f_ref[...]
    y = f * lax.rsqrt(jnp.mean(f * f, axis=-1, keepdims=True) + NORM_EPS) * gpost_ref[...]
    xn = x_ref[...] + y
    if has_next:
        gnext_ref, xo_ref, ho_ref = rest
        xo_ref[...] = xn
        h = xn * lax.rsqrt(jnp.mean(xn * xn, axis=-1, keepdims=True) + NORM_EPS) * gnext_ref[...]
        ho_ref[...] = h.astype(ho_ref.dtype)
    else:
        (xo_ref,) = rest
        xo_ref[...] = xn


def _norm_residual(f, x, g_post, g_next):
    m, d = x.shape
    tm = _tile(m, 256)
    row = pl.BlockSpec((tm, d), lambda i: (i, 0))
    vec = pl.BlockSpec((1, d), lambda i: (0, 0))
    has_next = g_next is not None
    ins = [f, x, g_post.reshape(1, d)] + ([g_next.reshape(1, d)] if has_next else [])
    out = pl.pallas_call(
        functools.partial(_norm_residual_kernel, has_next=has_next),
        grid=(m // tm,),
        in_specs=[row, row, vec] + ([vec] if has_next else []),
        out_specs=[row, row] if has_next else [row],
        out_shape=([jax.ShapeDtypeStruct((m, d), F32), jax.ShapeDtypeStruct((m, d), BF16)]
                   if has_next else [jax.ShapeDtypeStruct((m, d), F32)]),
        compiler_params=_cparams(("parallel",)),
        name="norm_residual",
    )(*ins)
    return (out[0], out[1]) if has_next else (out[0], None)


def _proj_kernel(*refs, rope, scale, tn, hw):
    if rope:
        a_ref, w_ref, cos_ref, sin_ref, *outs = refs
    else:
        a_ref, w_ref, *outs = refs
    acc = jnp.dot(a_ref[...], w_ref[...], preferred_element_type=F32)
    if rope:
        cos = cos_ref[...]
        sin = sin_ref[...]
    for g in range(tn // hw):
        x = acc[:, g * hw:(g + 1) * hw]
        if rope:
            x = x * cos + pltpu.roll(x, hw // 2, 1) * sin
        if scale != 1.0:
            x = x * scale
        for o in outs:
            if len(o.shape) == 3:
                o[:, g, :] = x.astype(o.dtype)
            else:
                o[:, g * hw:(g + 1) * hw] = x.astype(o.dtype)


def _proj(a, w, outs, cos=None, sin=None, scale=1.0, hw=None, tm=MM_TILE, tn=MM_TILE):
    m, k = a.shape
    n = w.shape[1]
    tm, tn = _tile(m, tm), _tile(n, tn)
    hw = tn if hw is None else hw
    rope = cos is not None
    w_mode = dict(pipeline_mode=pl.Buffered(1)) if n == tn else {}
    in_specs = [pl.BlockSpec((tm, k), lambda i, j: (i, 0)), pl.BlockSpec((k, tn), lambda i, j: (0, j), **w_mode)]
    ins = [a, w]
    if rope:
        in_specs += [pl.BlockSpec((tm, hw), lambda i, j: (i, 0))] * 2
        ins += [cos, sin]
    out_specs, out_shape = [], []
    for dt, head_major in outs:
        if head_major:
            out_specs.append(pl.BlockSpec((tm, tn // hw, hw), lambda i, j: (i, j, 0)))
            out_shape.append(jax.ShapeDtypeStruct((m, n // hw, hw), dt))
        else:
            out_specs.append(pl.BlockSpec((tm, tn), lambda i, j: (i, j)))
            out_shape.append(jax.ShapeDtypeStruct((m, n), dt))
    return pl.pallas_call(
        functools.partial(_proj_kernel, rope=rope, scale=scale, tn=tn, hw=hw),
        grid=(m // tm, n // tn),
        in_specs=in_specs,
        out_specs=out_specs,
        out_shape=out_shape,
        compiler_params=_cparams(("parallel", "parallel")),
        name="proj_rope" if rope else "proj",
    )(*ins)


def _dense_kernel(*refs, fn, n_rows, n_w, n_main_out, has_side):
    refs = list(refs)
    main_rows = refs[:n_rows]
    side_rows = refs[n_rows:2 * n_rows] if has_side else []
    pos = n_rows * (2 if has_side else 1)
    w_refs = refs[pos:pos + n_w]
    main_outs = refs[pos + n_w:pos + n_w + n_main_out]
    side_outs = refs[pos + n_w + n_main_out:]
    wv = [w[...].astype(BF16) for w in w_refs]
    fn(main_rows, wv, main_outs)
    if has_side:
        @pl.when(pl.program_id(0) == 0)
        def _():
            fn(side_rows, wv, side_outs)


def _dense(fn, row_ins, w_ins, out_defs, *, n, tn, k_steps=1, name):
    m = row_ins[0][0].shape[0]
    has_side = row_ins[0][1] is not None
    tm = _tile(m, MM_TILE)
    nj = n // tn
    in_specs, ins = [], []
    for which in ((0, 1) if has_side else (0,)):
        for entry in row_ins:
            a, k_tiled = entry[which], entry[2]
            rows = tm if which == 0 else a.shape[0]
            width = a.shape[1] // k_steps if k_tiled else a.shape[1]
            if which == 0:
                imap = (lambda i, j, kk: (i, kk)) if k_tiled else (lambda i, j, kk: (i, 0))
            else:
                imap = (lambda i, j, kk: (0, kk)) if k_tiled else (lambda i, j, kk: (0, 0))
            in_specs.append(pl.BlockSpec((rows, width), imap))
            ins.append(a)
    for w, layer, k_rows, rb, cb in w_ins:
        in_specs.append(pl.BlockSpec((None, k_rows, tn),
                                     lambda i, j, kk, layer=layer, rb=rb, cb=cb: (layer, rb + kk, cb + j)))
        ins.append(w)
    out_specs, out_shape = [], []
    for dt, hw, _ in out_defs:
        if hw is None:
            out_specs.append(pl.BlockSpec((tm, tn), lambda i, j, kk: (i, j)))
            out_shape.append(jax.ShapeDtypeStruct((m, n), dt))
        else:
            out_specs.append(pl.BlockSpec((tm, tn // hw, hw), lambda i, j, kk: (i, j, 0)))
            out_shape.append(jax.ShapeDtypeStruct((m, n // hw, hw), dt))
    n_main_out = len(out_specs)
    if has_side:
        ms = row_ins[0][1].shape[0]
        for _, _, side_dt in out_defs:
            if side_dt is not None:
                out_specs.append(pl.BlockSpec((ms, tn), lambda i, j, kk: (0, jnp.where(i == 0, j, nj - 1))))
                out_shape.append(jax.ShapeDtypeStruct((ms, n), side_dt))
    out = pl.pallas_call(
        functools.partial(_dense_kernel, fn=fn, n_rows=len(row_ins), n_w=len(w_ins), n_main_out=n_main_out,
                          has_side=has_side),
        grid=(m // tm, nj, k_steps),
        in_specs=in_specs,
        out_specs=out_specs,
        out_shape=out_shape,
        compiler_params=_cparams(("arbitrary", "arbitrary", "arbitrary")),
        name=name,
    )(*ins)
    return out[:n_main_out], out[n_main_out:]


def _proj_fn(rows, wv, outs, *, rope, scale, hw):
    acc = jnp.dot(rows[0][...], wv[0], preferred_element_type=F32)
    if rope:
        cos = rows[1][...]
        sin = rows[2][...]
    for g in range(acc.shape[1] // hw):
        x = acc[:, g * hw:(g + 1) * hw]
        if rope:
            x = x * cos + pltpu.roll(x, hw // 2, 1) * sin
        if scale != 1.0:
            x = x * scale
        for o in outs:
            if len(o.shape) == 3:
                o[:, g, :] = x.astype(o.dtype)
            else:
                o[:, g * hw:(g + 1) * hw] = x.astype(o.dtype)


def _mm2_fn(rows, wv, outs):
    outs[0][...] = (jnp.dot(rows[0][...], wv[0], preferred_element_type=F32)
                    + jnp.dot(rows[1][...], wv[1], preferred_element_type=F32))


def _ffn_up_fn(rows, wv, outs):
    u = jnp.maximum(jnp.dot(rows[0][...], wv[0], preferred_element_type=F32), 0.0)
    outs[0][...] = (u * u).astype(outs[0].dtype)


def _mm_acc_fn(rows, wv, outs):
    part = jnp.dot(rows[0][...], wv[0], preferred_element_type=F32)

    @pl.when(pl.program_id(2) == 0)
    def _():
        outs[0][...] = part

    @pl.when(pl.program_id(2) != 0)
    def _():
        outs[0][...] += part


def _ple_fn(rows, wv, outs):
    gate = jnp.dot(rows[0][...], wv[0], preferred_element_type=F32)
    gate = 1.0 / (1.0 + jnp.exp(-gate))
    proj = jnp.dot(rows[1][...].astype(BF16), wv[1], preferred_element_type=F32)
    outs[0][...] = gate * proj


def _lambda(lq1_ref, lk1_ref, lq2_ref, lk2_ref, lam_init):
    s1 = jnp.sum(lq1_ref[...] * lk1_ref[...], axis=-1, keepdims=True)
    s2 = jnp.sum(lq2_ref[...] * lk2_ref[...], axis=-1, keepdims=True)
    return jnp.exp(s1) - jnp.exp(s2) + lam_init


def _subln(o, g, lam_init):
    y = o * lax.rsqrt(jnp.mean(o * o, axis=-1, keepdims=True) + SUBLN_EPS)
    return y * g * (1.0 - lam_init)


def _lane_tile(x, width):
    return jnp.concatenate([x] * (width // LANES), axis=1)


def _prompt_attn_kernel(it_ref, jt_ref, q_ref, k_ref, v_ref, lq1_ref, lk1_ref, lq2_ref, lk2_ref, g_ref,
                        o_ref, s_sc, p_sc, a_sc, m_sc, l_sc, acc_sc, *, lam_init, dk, strip):
    s = pl.program_id(1)
    i = it_ref[s]
    j = jt_ref[s]
    t = q_ref.shape[0]
    dv = v_ref.shape[1]

    @pl.when(j == 0)
    def _():
        m_sc[...] = jnp.full(m_sc.shape, NEG_INF, F32)
        l_sc[...] = jnp.zeros(l_sc.shape, F32)
        acc_sc[...] = jnp.zeros(acc_sc.shape, F32)

    def step(masked):
        for mp in range(2):
            s_sc[mp] = _dot_nt(q_ref[:, mp * dk:(mp + 1) * dk], k_ref[:, mp * dk:(mp + 1) * dk])

        v = v_ref[...]
        for mp in range(2):
            for r in range(t // strip):
                rows = slice(r * strip, (r + 1) * strip)
                sc = s_sc[mp, rows, :]
                if masked:
                    sc = jnp.where(_iota(sc.shape, 1) <= _iota(sc.shape, 0) + r * strip, sc, NEG_INF)
                m_prev = m_sc[mp, rows, :]
                m_new = jnp.maximum(m_prev, jnp.max(sc, axis=-1, keepdims=True))
                alpha = jnp.exp2(m_prev - m_new)
                p = jnp.exp2(sc - _lane_tile(m_new, t))
                l_sc[mp, rows, :] = alpha * l_sc[mp, rows, :] + jnp.sum(p, axis=-1, keepdims=True)
                a_sc[mp, rows, :] = alpha
                m_sc[mp, rows, :] = m_new
                p_sc[mp, rows, :] = p.astype(BF16)
            acc_sc[mp] = (_lane_tile(a_sc[mp], dv) * acc_sc[mp]
                          + jnp.dot(p_sc[mp], v, preferred_element_type=F32))

    @pl.when(j < i)
    def _():
        step(False)

    @pl.when(j == i)
    def _():
        step(True)
        lam = _lambda(lq1_ref, lk1_ref, lq2_ref, lk2_ref, lam_init)
        o = acc_sc[0] / _lane_tile(l_sc[0], dv) - lam * (acc_sc[1] / _lane_tile(l_sc[1], dv))
        o_ref[...] = _subln(o, g_ref[...], lam_init).astype(o_ref.dtype)


def _prompt_attention(q, k, v, lam_vecs, g, lam_init, heads, dk, dv):
    s_len = q.shape[0]
    t = _tile(s_len, ATTN_TILE)
    nb = s_len // t
    pairs = [(i, j) for i in range(nb) for j in range(i + 1)]
    it = jnp.asarray([p[0] for p in pairs], jnp.int32)
    jt = jnp.asarray([p[1] for p in pairs], jnp.int32)
    vec = pl.BlockSpec((1, dk), lambda h, s, it, jt: (0, 0))
    grid_spec = pltpu.PrefetchScalarGridSpec(
        num_scalar_prefetch=2,
        grid=(heads, len(pairs)),
        in_specs=[pl.BlockSpec((t, 2 * dk), lambda h, s, it, jt: (it[s], h)),
                  pl.BlockSpec((t, 2 * dk), lambda h, s, it, jt: (jt[s], h)),
                  pl.BlockSpec((t, dv), lambda h, s, it, jt: (jt[s], h)),
                  vec, vec, vec, vec,
                  pl.BlockSpec((1, dv), lambda h, s, it, jt: (0, 0))],
        out_specs=pl.BlockSpec((t, dv), lambda h, s, it, jt: (it[s], h)),
        scratch_shapes=[pltpu.VMEM((2, t, t), F32), pltpu.VMEM((2, t, t), BF16), pltpu.VMEM((2, t, LANES), F32),
                        pltpu.VMEM((2, t, LANES), F32), pltpu.VMEM((2, t, LANES), F32),
                        pltpu.VMEM((2, t, dv), F32)],
    )
    return pl.pallas_call(
        functools.partial(_prompt_attn_kernel, lam_init=lam_init, dk=dk, strip=min(ATTN_STRIP, t)),
        grid_spec=grid_spec,
        out_shape=jax.ShapeDtypeStruct((s_len, heads * dv), BF16),
        compiler_params=_cparams(("parallel", "arbitrary")),
        name="prompt_attention",
    )(it, jt, q, k, v, *lam_vecs, g.reshape(1, dv))


def _sample_attn_kernel(pt_ref, q_ref, kn_ref, vn_ref, *rest, lam_init, heads, dk, dv, pages):
    k_refs = rest[:pages]
    v_refs = rest[pages:2 * pages]
    (lq1_ref, lk1_ref, lq2_ref, lk2_ref, g_ref, o_ref, q_sc, bias_sc, m_sc, l_sc, acc_sc) = rest[2 * pages:]
    step_id = pl.program_id(1)
    half = SUBLANES // 2
    rp = heads * half
    page_lanes = bias_sc.shape[1]

    def own_head(shape):
        return _iota(shape, 1) % heads == (_iota(shape, 0) % rp) // half

    @pl.when(step_id == 0)
    def _():
        q = q_ref[0:SUBLANES, :]
        for par in range(2):
            for j in range(heads // 2):
                lo = q[:, (4 * j + par) * dk:(4 * j + par + 1) * dk]
                hi = q[:, (4 * j + 2 + par) * dk:(4 * j + 3 + par) * dk]
                q_sc[par * rp + j * SUBLANES:par * rp + (j + 1) * SUBLANES, :] = lo + pltpu.roll(hi, half, 0)
        bias_sc[...] = jnp.where(own_head(bias_sc.shape), 0.0, NEG_INF)
        m_sc[...] = jnp.full(m_sc.shape, NEG_INF, F32)
        l_sc[...] = jnp.zeros(l_sc.shape, F32)
        acc_sc[...] = jnp.zeros(acc_sc.shape, F32)

    def scores(k_ref, lanes):
        return jnp.concatenate(
            [_dot_nt(q_sc[par * rp:(par + 1) * rp, :], k_ref[pl.ds(par, lanes, stride=2), :]) for par in range(2)],
            axis=0)

    def update(sc, v_list):
        m_prev = m_sc[...]
        m_new = jnp.maximum(m_prev, jnp.max(sc, axis=-1, keepdims=True))
        alpha = jnp.exp2(m_prev - m_new)
        p = jnp.exp2(sc - m_new)
        l_sc[...] = alpha * l_sc[...] + jnp.sum(p, axis=-1, keepdims=True)
        pv, off = None, 0
        for vr in v_list:
            n = vr.shape[0]
            part = _dot(p[:, off:off + n], vr[...])
            pv = part if pv is None else pv + part
            off += n
        acc_sc[...] = alpha * acc_sc[...] + pv
        m_sc[...] = m_new

    bias = bias_sc[...]
    update(jnp.concatenate([scores(kr, page_lanes) + bias for kr in k_refs], axis=1), v_refs)

    @pl.when(step_id == pl.num_programs(1) - 1)
    def _():
        sc = scores(kn_ref, vn_ref.shape[0])
        ok = own_head(sc.shape) & (_iota(sc.shape, 1) // heads <= _iota(sc.shape, 0) % half)
        update(jnp.where(ok, sc, NEG_INF), [vn_ref])
        lam = _lambda(lq1_ref, lk1_ref, lq2_ref, lk2_ref, lam_init)
        accn = acc_sc[...] / l_sc[...]
        y = _subln(accn[:rp] - lam * accn[rp:], g_ref[...], lam_init)
        for j in range(heads // 2):
            y8 = y[j * SUBLANES:(j + 1) * SUBLANES]
            o_ref[:, (2 * j) * dv:(2 * j + 1) * dv] = y8
            o_ref[:, (2 * j + 1) * dv:(2 * j + 2) * dv] = pltpu.roll(y8, half, 0)


def _sample_attention(q, k_new, v_new, cache_k, cache_v, pool_offset, page_table, lam_vecs, g, lam_init, n_new):
    b, pad_t, width = q.shape
    n_pages = page_table.shape[1]
    dk, dv = cache_k.shape[2], cache_v.shape[2]
    maps = width // dk
    heads = maps // 2
    page = cache_k.shape[1] // maps
    half = SUBLANES // 2
    assert n_new <= half and pad_t >= SUBLANES and heads % 2 == 0
    pages = math.gcd(n_pages, PAGES_PER_STEP)
    rows = 2 * heads * half
    vec = pl.BlockSpec((1, dk), lambda bi, s, pt: (0, 0))
    per_row = lambda a: pl.BlockSpec((None,) + a.shape[1:], lambda bi, s, pt: (bi, 0, 0))

    def page_spec(pg, a):
        return pl.BlockSpec((None,) + a.shape[1:], lambda bi, s, pt: (pool_offset + pt[bi, s * pages + pg], 0, 0))

    grid_spec = pltpu.PrefetchScalarGridSpec(
        num_scalar_prefetch=1,
        grid=(b, n_pages // pages),
        in_specs=([per_row(q), per_row(k_new), per_row(v_new)] + [page_spec(pg, cache_k) for pg in range(pages)]
                  + [page_spec(pg, cache_v) for pg in range(pages)]
                  + [vec, vec, vec, vec, pl.BlockSpec((1, dv), lambda bi, s, pt: (0, 0))]),
        out_specs=pl.BlockSpec((None, SUBLANES, heads * dv), lambda bi, s, pt: (bi, 0, 0)),
        scratch_shapes=[pltpu.VMEM((rows, dk), F32), pltpu.VMEM((rows, page * heads), F32),
                        pltpu.VMEM((rows, 1), F32), pltpu.VMEM((rows, 1), F32), pltpu.VMEM((rows, dv), F32)],
    )
    return pl.pallas_call(
        functools.partial(_sample_attn_kernel, lam_init=lam_init, heads=heads, dk=dk, dv=dv, pages=pages),
        grid_spec=grid_spec,
        out_shape=jax.ShapeDtypeStruct((b, SUBLANES, heads * dv), F32),
        compiler_params=_cparams(("parallel", "arbitrary")),
        name="sample_attention",
    )(page_table, q, k_new, v_new, *([cache_k] * pages), *([cache_v] * pages), *lam_vecs, g.reshape(1, dv))


def _segsum(x, ones_bd):
    hi = x.astype(BF16)
    lo = (x - hi.astype(F32)).astype(BF16)
    return (jnp.dot(hi, ones_bd, preferred_element_type=F32) + jnp.dot(lo, ones_bd, preferred_element_type=F32))


def _block_ones(w):
    return jnp.where(_iota((w, w), 0) // RW_N == _iota((w, w), 1) // RW_N, 1.0, 0.0).astype(BF16)


def _rw_prep_kernel(z_ref, zprev_ref, mu_ref, w0_ref, w2_ref, a0_ref, a2_ref, g2_ref, kk_ref, ka_ref,
                    r_out, lw_out, k_out, v_out, kk_out, b_out, g_out, carry_sc, *, mode, period, rw, dd, da, gp):
    z = z_ref[...]
    tm = z.shape[0]
    rolled = pltpu.roll(z, 1, 0)
    row = _iota(z.shape, 0)
    if mode == "carry":
        @pl.when(pl.program_id(0) == 0)
        def _():
            carry_sc[...] = zprev_ref[...]
        zs = jnp.where(row == 0, carry_sc[...], rolled)
        carry_sc[...] = z[tm - 1:tm, :]
    else:
        zs = jnp.where(row % period == 0, zprev_ref[...], rolled)
    zz = z + (zs - z) * mu_ref[...]
    r = zz[:, :rw]
    kr = zz[:, rw:2 * rw]
    v = zz[:, 2 * rw:3 * rw]
    zw = zz[:, 3 * rw:3 * rw + dd]
    za = zz[:, 3 * rw + dd:3 * rw + dd + da]
    zg = zz[:, 3 * rw + dd + da:3 * rw + dd + da + gp]
    u = w0_ref[...] + _dot(jnp.tanh(zw), w2_ref[...])
    w_log = jnp.minimum(u, 0.0) - jnp.log(1.0 + jnp.exp(-jnp.abs(u))) - 0.5
    lw_out[...] = -jnp.exp(w_log)
    a = 1.0 / (1.0 + jnp.exp(-(a0_ref[...] + _dot(za, a2_ref[...]))))
    g_out[...] = _dot(1.0 / (1.0 + jnp.exp(-zg)), g2_ref[...])
    ones_bd = _block_ones(LANES)
    kk = kr * kk_ref[...]
    for c in range(rw // LANES):
        sl = slice(c * LANES, (c + 1) * LANES)
        kc = kk[:, sl]
        kc = kc * lax.rsqrt(jnp.maximum(_segsum(kc * kc, ones_bd), 1e-24))
        kk_out[:, sl] = kc
        b_out[:, sl] = kc * a[:, sl]
    r_out[...] = r
    v_out[...] = v
    k_out[...] = kr * (1.0 + (a - 1.0) * ka_ref[...])


def _rw_prep(z, zprev, prm, mode, period):
    m, zp = z.shape
    rw, dd, da, gp = prm["rw"], prm["dd"], prm["da"], prm["gp"]
    tm = _tile(m, 128)
    row = pl.BlockSpec((tm, zp), lambda i: (i, 0))
    full = lambda a: pl.BlockSpec(a.shape, lambda i: (0,) * a.ndim)
    zprev_spec = full(zprev) if mode == "carry" else row
    params = [prm["mu"], prm["w0"], prm["w2"], prm["a0"], prm["a2"], prm["g2"], prm["k_k"], prm["k_a"]]
    out_spec = pl.BlockSpec((tm, rw), lambda i: (i, 0))
    return pl.pallas_call(
        functools.partial(_rw_prep_kernel, mode=mode, period=period, rw=rw, dd=dd, da=da, gp=gp),
        grid=(m // tm,),
        in_specs=[row, zprev_spec] + [full(p) for p in params],
        out_specs=[out_spec] * 7,
        out_shape=[jax.ShapeDtypeStruct((m, rw), F32)] * 7,
        scratch_shapes=[pltpu.VMEM((1, zp), F32)],
        compiler_params=_cparams(("arbitrary",)),
        name="rwkv_prep",
    )(z, zprev, *params)


def _bd(x, gw):
    head = (_iota(x.shape, 1) % gw) // RW_N
    return jnp.concatenate([jnp.where(head == h, x, 0.0) for h in range(gw // RW_N)], axis=0)


def _rw_scan_kernel(*refs, has_s0, t_valid, gw, n_groups):
    if has_s0:
        (r_ref, lw_ref, k_ref, v_ref, kk_ref, b_ref, g_ref, rk_ref, lng_ref, lnb_ref, s0_ref,
         y_out, s_out, sb_sc) = refs
    else:
        (r_ref, lw_ref, k_ref, v_ref, kk_ref, b_ref, g_ref, rk_ref, lng_ref, lnb_ref,
         y_out, s_out, sb_sc) = refs
    c = RW_CHUNK
    n = RW_N
    tc = r_ref.shape[0]
    ci = pl.program_id(1)
    last = pl.num_programs(1) - 1

    tri_inc = jnp.where(_iota((c, c), 1) <= _iota((c, c), 0), 1.0, 0.0).astype(BF16)
    s_idx = _iota((c, gw), 1) % n
    t_idx = _iota((c, gw), 0)
    strict = s_idx < t_idx
    incl = s_idx <= t_idx
    eye_w = jnp.where(s_idx == t_idx, 1.0, 0.0)
    gi0, gi1 = _iota((gw, gw), 0), _iota((gw, gw), 1)
    blk = gi0 // n == gi1 // n
    eye_g = jnp.where(gi0 == gi1, 1.0, 0.0)
    ones_bd = jnp.where(blk, 1.0, 0.0).astype(BF16)
    sel = jnp.where(_iota((gw, n), 0) % n == _iota((gw, n), 1), 1.0, 0.0).astype(BF16)
    sel_t = jnp.where(_iota((n, gw), 1) % n == _iota((n, gw), 0), 1.0, 0.0).astype(BF16)
    valid = _iota((tc, gw), 0) < t_valid

    def load(ref, sl, fill=0.0):
        x = ref[:, sl]
        if t_valid < tc:
            x = jnp.where(valid, x, fill)
        if tc < c:
            x = jnp.concatenate([x, jnp.full((c - tc, gw), fill, F32)], axis=0)
        return x

    @pl.when(ci == 0)
    def _():
        for gi in range(n_groups):
            if has_s0:
                sb_sc[gi] = jnp.where(blk, _dot_sel_r(s0_ref[gi * gw:(gi + 1) * gw, :], sel_t), 0.0)
            else:
                sb_sc[gi] = jnp.zeros((gw, gw), F32)

    def group_chain(gi):
        sl = slice(gi * gw, (gi + 1) * gw)
        r, lw, k, v, kk, b = (load(ref, sl) for ref in (r_ref, lw_ref, k_ref, v_ref, kk_ref, b_ref))
        cl = _dot_sel(tri_inc, lw)
        yield
        cend = cl[c - 1:c, :]
        kt = kk * jnp.exp(cl - lw)
        rt = r * jnp.exp(cl)
        g_inv = jnp.exp(-cl)
        g_rat = jnp.exp(cend - cl)
        ktil, btil = k * g_inv, b * g_inv
        khat, bhat = k * g_rat, b * g_rat
        la = _dot_nt(jnp.concatenate([kt, rt], axis=0),
                     jnp.concatenate([_bd(btil, gw), _bd(ktil, gw)], axis=0))
        yield
        lb = jnp.where(strict, la[:c, :gw], 0.0)
        lk = jnp.where(strict, la[:c, gw:], 0.0)
        ab = jnp.where(incl, la[c:, :gw], 0.0)
        ak = jnp.where(incl, la[c:, gw:], 0.0)
        bdv = _bd(v, gw)
        lkv = _dot(lk, bdv)
        y0 = _dot(ak, bdv)
        qt = _dot_tn(v, khat)
        x = eye_w - lb
        p = lb
        for _ in range(max(0, math.ceil(math.log2(t_valid)) - 1)):
            p = _dot(p, _bd(p, gw))
            yield
            x = x + _dot(x, _bd(p, gw))
            yield
        tw = _dot(x, _bd(jnp.concatenate([kt, lkv], axis=1), gw))
        yield
        abw = _dot(ab, _bd(tw, gw))
        tn = _dot_tn(tw, bhat)
        yield
        rr = rt - abw[:, :gw]
        y0 = y0 - abw[:, gw:]
        pt = eye_g * jnp.exp(cend) - jnp.where(blk, tn[:gw], 0.0)
        qt = jnp.where(blk, qt - tn[gw:], 0.0)
        s_old = sb_sc[gi]
        y = _dot_nt(rr, s_old) + y0
        sb_sc[gi] = _dot(s_old, pt) + qt
        bonus = _segsum(r * k * rk_ref[:, sl], ones_bd) * v
        yield
        mean = _segsum(y, ones_bd) * (1.0 / n)
        yield
        d = y - mean
        var = _segsum(d * d, ones_bd) * (1.0 / n)
        yield
        yn = d * lax.rsqrt(var + RW_LN_EPS) * lng_ref[:, sl] + lnb_ref[:, sl]
        out = (yn + bonus)[:tc] * g_ref[:, sl]
        y_out[:, sl] = out.astype(y_out.dtype)

    chains = [group_chain(gi) for gi in range(n_groups)]
    while chains:
        chains = [ch for ch in chains if next(ch, "done") != "done"]

    @pl.when(ci == last)
    def _():
        for gi in range(n_groups):
            s_out[gi * gw:(gi + 1) * gw, :] = _dot_sel_r(sb_sc[gi], sel)


def _rw_scan(r, lw, k, v, kk, b, g, prm, s0, n_seq, tc, t_valid):
    m, rw = r.shape
    n_chunks = m // (n_seq * tc)
    gw = math.gcd(rw, RW_GROUP)
    n_groups = rw // gw
    row = pl.BlockSpec((tc, rw), lambda s, ci: (s * n_chunks + ci, 0))
    vec = pl.BlockSpec((1, rw), lambda s, ci: (0, 0))
    state = pl.BlockSpec((None, rw, RW_N), lambda s, ci: (s, 0, 0))
    has_s0 = s0 is not None
    ins = [r, lw, k, v, kk, b, g, prm["r_k"], prm["ln_g"], prm["ln_b"]] + ([s0] if has_s0 else [])
    return pl.pallas_call(
        functools.partial(_rw_scan_kernel, has_s0=has_s0, t_valid=t_valid, gw=gw, n_groups=n_groups),
        grid=(n_seq, n_chunks),
        in_specs=[row] * 7 + [vec] * 3 + ([state] if has_s0 else []),
        out_specs=[row, state],
        out_shape=[jax.ShapeDtypeStruct((m, rw), BF16), jax.ShapeDtypeStruct((n_seq, rw, RW_N), F32)],
        scratch_shapes=[pltpu.VMEM((n_groups, gw, gw), F32)],
        compiler_params=_cparams(("parallel", "arbitrary")),
        name="rwkv_scan",
    )(*ins)


def _rope_tables(pos, dk):
    half = dk // 2
    inv = ROPE_THETA ** (-jnp.arange(half, dtype=F32) / half)
    ang = pos.astype(F32)[:, None] * inv[None, :]
    cos, sin = jnp.cos(ang), jnp.sin(ang)
    return jnp.concatenate([cos, cos], axis=-1), jnp.concatenate([-sin, sin], axis=-1)


def _prep_weights(i, dims, w_in, rw_mu, rw_w0, rw_w2, rw_a0, rw_a2, rw_g2, rw_k_k, rw_k_a, rw_r_k, rw_ln_g,
                  rw_ln_b):
    qw, da_w, rw, dd, da, dg, gp, zp = (dims[n] for n in ("qw", "da_w", "rw", "dd", "da", "dg", "gp", "zp"))
    rw_cols = 3 * rw + dd + da + dg
    win = w_in[i]
    w = {
        "k": win[:, qw:2 * qw].astype(BF16),
        "v": win[:, 2 * qw:2 * qw + da_w].astype(BF16),
        "z": jnp.pad(win[:, 2 * qw + da_w:], ((0, 0), (0, zp - rw_cols))).astype(BF16)[None],
    }
    prm = {
        "rw": rw, "dd": dd, "da": da, "gp": gp,
        "mu": jnp.pad(rw_mu[i], (0, zp - rw_cols)).reshape(1, zp),
        "w0": rw_w0[i].reshape(1, rw), "w2": rw_w2[i].astype(BF16),
        "a0": rw_a0[i].reshape(1, rw), "a2": rw_a2[i].astype(BF16),
        "g2": jnp.pad(rw_g2[i], ((0, gp - dg), (0, 0))).astype(BF16),
        "k_k": rw_k_k[i].reshape(1, rw), "k_a": rw_k_a[i].reshape(1, rw),
        "r_k": rw_r_k[i].reshape(1, rw), "ln_g": rw_ln_g[i].reshape(1, rw), "ln_b": rw_ln_b[i].reshape(1, rw),
    }
    return w, prm


def _tail(i, xs, os_, rwos, pes, w_out, w_up, w_down, w_ple_gate, w_ple_proj, gains):
    g_post_mix, g_pre_ffn, g_post_ffn, g_pre_ple, g_post_ple = gains
    d = xs[0].shape[1]
    d_ff = w_up.shape[2]
    half = os_[0].shape[1]
    assert rwos[0].shape[1] == half and w_out.shape[1] == 2 * half
    tn = _tile(d, DENSE_TN)
    mix = _dense(_mm2_fn, [(os_[0], os_[1], False), (rwos[0], rwos[1], False)],
                 [(w_out, i, half, 0, 0), (w_out, i, half, 1, 0)], [(F32, None, F32)], n=d, tn=tn, name="out_proj")
    x, h = zip(*(_norm_residual(f[0], xx, g_post_mix, g_pre_ffn) for f, xx in zip(mix, xs)))
    u = _dense(_ffn_up_fn, [(h[0], h[1], False)], [(w_up, i, d, 0, 0)], [(BF16, None, BF16)],
               n=d_ff, tn=_tile(d_ff, DENSE_TN), name="ffn_up")
    tk = _tile(d_ff, FFN_TK)
    f = _dense(_mm_acc_fn, [(u[0][0], u[1][0], True)], [(w_down, i, tk, 0, 0)], [(F32, None, F32)],
               n=d, tn=tn, k_steps=d_ff // tk, name="ffn_down")
    x, h = zip(*(_norm_residual(ff[0], xx, g_post_ffn, g_pre_ple) for ff, xx in zip(f, x)))
    e = _dense(_ple_fn, [(h[0], h[1], False), (pes[0], pes[1], False)],
               [(w_ple_gate, i, d, 0, 0), (w_ple_proj, i, pes[0].shape[1], 0, 0)], [(F32, None, F32)],
               n=d, tn=tn, name="ple")
    return tuple(_norm_residual(ee[0], xx, g_post_ple, None)[0] for ee, xx in zip(e, x))


def kernel(x_prompt, x_sample, cache_k, cache_v, state_wkv, state_shift, page_table, p_prompt, p_sample, g_pre_mix, w_in, lam_q1, lam_k1, lam_q2, lam_k2, da_subln_g, rw_mu, rw_w0, rw_w2, rw_a0, rw_a2, rw_g2, rw_k_k, rw_k_a, rw_r_k, rw_ln_g, rw_ln_b, w_out, g_post_mix, g_pre_ffn, w_up, w_down, g_post_ffn, g_pre_ple, w_ple_gate, w_ple_proj, g_post_ple):
    batch, seq, d = x_prompt.shape
    db, dseq, _ = x_sample.shape
    depth, n_pool, page, maps, dk = cache_k.shape
    heads, dv = cache_v.shape[3], cache_v.shape[4]
    rwh, rwn = rw_r_k.shape[1], rw_r_k.shape[2]
    assert maps == 2 * heads and dk == LANES and rwn == RW_N and seq % RW_CHUNK == 0
    qw, da_w, rw = maps * dk, heads * dv, rwh * rwn
    dd, da, dg = rw_w2.shape[1], rw_a2.shape[1], rw_g2.shape[1]
    assert dd % LANES == 0 and da % LANES == 0 and rw % LANES == 0
    gp = _round_up(dg, LANES)
    rw_cols = 3 * rw + dd + da + dg
    zp = _round_up(3 * rw + dd + da + gp, min(DENSE_TN, _round_up(rw_cols, LANES)))
    dims = dict(qw=qw, da_w=da_w, rw=rw, dd=dd, da=da, dg=dg, gp=gp, zp=zp)
    n_pages = page_table.shape[1]
    past_len = n_pages * page
    pad_t = _round_up(dseq, SUBLANES)
    pad_a = _round_up(dseq, 2 * SUBLANES)
    scale = dk ** -0.5

    cos_p, sin_p = _rope_tables(jnp.tile(jnp.arange(seq, dtype=jnp.int32), batch), dk)
    cos_s, sin_s = _rope_tables(jnp.tile(past_len + jnp.arange(dseq, dtype=jnp.int32), db), dk)

    xp = x_prompt.reshape(batch * seq, d)
    xs = x_sample.reshape(db * dseq, d)
    outs = [[] for _ in range(8)]
    for i in range(depth):
        lam_init = 0.8 - 0.6 * math.exp(-0.3 * i)
        w, prm = _prep_weights(i, dims, w_in, rw_mu, rw_w0, rw_w2, rw_a0, rw_a2, rw_g2, rw_k_k, rw_k_a, rw_r_k,
                               rw_ln_g, rw_ln_b)
        lam_vecs = [v[i].reshape(1, dk) for v in (lam_q1, lam_k1, lam_q2, lam_k2)]
        gains = (g_post_mix[i], g_pre_ffn[i], g_post_ffn[i], g_pre_ple[i], g_post_ple[i])

        h = _rmsnorm(xp, g_pre_mix[i], NORM_EPS)
        h_s = _rmsnorm(xs, g_pre_mix[i], NORM_EPS)
        (q,), (q_s,) = _dense(
            functools.partial(_proj_fn, rope=True, scale=scale * math.log2(math.e), hw=dk),
            [(h, h_s, False), (cos_p, cos_s, False), (sin_p, sin_s, False)], [(w_in, i, d, 0, 0)],
            [(BF16, None, F32)], n=qw, tn=_tile(qw, DENSE_TN), name="proj_q")
        tn_z = _tile(zp, DENSE_TN)
        (z,), (z_s,) = _dense(functools.partial(_proj_fn, rope=False, scale=1.0, hw=tn_z), [(h, h_s, False)],
                              [(w["z"], 0, d, 0, 0)], [(F32, None, F32)], n=zp, tn=tn_z, name="proj_z")
        k32, k16 = _proj(h, w["k"], [(F32, True), (BF16, False)], cos_p, sin_p, hw=dk)
        v32, v16 = _proj(h, w["v"], [(F32, True), (BF16, False)], hw=dv, tm=MM_TILE // 2, tn=da_w)

        o_rows, rw_rows, wkv_rows = [], [], []
        for bi in range(batch):
            rs = slice(bi * seq, (bi + 1) * seq)
            o_rows.append(_prompt_attention(q[rs], k16[rs], v16[rs], lam_vecs, da_subln_g[i], lam_init,
                                            heads, dk, dv))
            mixer_in = _rw_prep(z[rs], jnp.zeros((1, zp), F32), prm, "carry", 0)
            rwo, s_fin = _rw_scan(*mixer_in, prm, None, 1, RW_CHUNK, RW_CHUNK)
            rw_rows.append(rwo)
            wkv_rows.append(s_fin.reshape(rwh, rwn, rwn))
        o = o_rows[0] if batch == 1 else jnp.concatenate(o_rows, axis=0)
        rwo = rw_rows[0] if batch == 1 else jnp.concatenate(rw_rows, axis=0)
        outs[0].append(k32.reshape(batch, seq, maps, dk))
        outs[1].append(v32.reshape(batch, seq, heads, dv))
        outs[4].append(jnp.stack(wkv_rows, 0))
        outs[6].append(z.reshape(batch, seq, zp)[:, -1, :rw_cols])

        (k32,) = _proj(h_s, w["k"], [(F32, False)], cos_s, sin_s, hw=dk)
        (v32,) = _proj(h_s, w["v"], [(F32, False)])
        pad3 = lambda a, t: jnp.pad(a.reshape(db, dseq, -1), ((0, 0), (0, t - dseq), (0, 0)))
        o_s = _sample_attention(pad3(q_s, pad_a), pad3(k32, pad_a).reshape(db, pad_a * maps, dk),
                                pad3(v32, pad_a).reshape(db, pad_a * heads, dv),
                                cache_k.reshape(depth * n_pool, page * maps, dk),
                                cache_v.reshape(depth * n_pool, page * heads, dv), i * n_pool, page_table,
                                lam_vecs, da_subln_g[i], lam_init, dseq)
        o_s = o_s[:, :dseq].reshape(db * dseq, da_w).astype(BF16)
        zprev = jnp.zeros((db, pad_t, zp), F32).at[:, 0, :rw_cols].set(state_shift[i])
        mixer_in = _rw_prep(pad3(z_s, pad_t).reshape(db * pad_t, zp), zprev.reshape(db * pad_t, zp), prm, "rows",
                            pad_t)
        rwo_s, s_fin = _rw_scan(*mixer_in, prm, state_wkv[i].reshape(db, rw, rwn), db, pad_t, dseq)
        rwo_s = rwo_s.reshape(db, pad_t, rw)[:, :dseq].reshape(db * dseq, rw)
        outs[2].append(k32.reshape(db, dseq, maps, dk))
        outs[3].append(v32.reshape(db, dseq, heads, dv))
        outs[5].append(s_fin.reshape(db, rwh, rwn, rwn))
        outs[7].append(z_s.reshape(db, dseq, zp)[:, -1, :rw_cols])

        xp, xs = _tail(i, (xp, xs), (o, o_s), (rwo, rwo_s),
                       (p_prompt[i].reshape(batch * seq, -1), p_sample[i].reshape(db * dseq, -1)),
                       w_out, w_up, w_down, w_ple_gate, w_ple_proj, gains)

    st = [jnp.stack(o, 0) for o in outs]
    return (xp.reshape(batch, seq, d), xs.reshape(db, dseq, d), st[0], st[1], st[2], st[3], st[4], st[5],
            st[6], st[7])
```

```python
import functools
import math

import jax
import jax.numpy as jnp
from jax import lax
from jax.experimental import pallas as pl
from jax.experimental.pallas import tpu as pltpu

F32 = jnp.float32
BF16 = jnp.bfloat16

LANES = 128
SUBLANES = 8
VMEM_LIMIT_BYTES = 56 * 1024 * 1024

ROPE_THETA = 10000.0
NORM_EPS = 1e-6
SUBLN_EPS = 1e-5
RW_LN_EPS = 64e-5
NEG_INF = -1e30

RW_N = 64
RW_CHUNK = 64
RW_GROUP = 256
MM_TILE = 1024
FFN_TK = 4096
ATTN_TILE = 1024
ATTN_STRIP = 32
PAGES_PER_STEP = 8


def _cparams(sem):
    return pltpu.CompilerParams(dimension_semantics=sem, vmem_limit_bytes=VMEM_LIMIT_BYTES)


def _round_up(x, m):
    return (x + m - 1) // m * m


def _tile(n, pref):
    t = min(n, pref)
    assert n % t == 0, (n, t)
    return t


def _dot(a, b):
    return jnp.dot(a.astype(BF16), b.astype(BF16), preferred_element_type=F32)


def _dot_nt(a, b):
    return lax.dot_general(a.astype(BF16), b.astype(BF16), (((1,), (1,)), ((), ())),
                           preferred_element_type=F32)


def _dot_tn(a, b):
    return lax.dot_general(a.astype(BF16), b.astype(BF16), (((0,), (0,)), ((), ())),
                           preferred_element_type=F32)


def _split3(x):
    hi = x.astype(BF16)
    r1 = x - hi.astype(F32)
    mid = r1.astype(BF16)
    lo = (r1 - mid.astype(F32)).astype(BF16)
    return hi, mid, lo


def _dot_sel(sel, x):
    hi, mid, lo = _split3(x)
    n = x.shape[1]
    y = jnp.dot(sel, jnp.concatenate([hi, mid, lo], axis=1), preferred_element_type=F32)
    return y[:, :n] + y[:, n:2 * n] + y[:, 2 * n:]


def _dot_sel_r(x, sel):
    hi, mid, lo = _split3(x)
    m = x.shape[0]
    y = jnp.dot(jnp.concatenate([hi, mid, lo], axis=0), sel, preferred_element_type=F32)
    return y[:m] + y[m:2 * m] + y[2 * m:]


def _iota(shape, dim):
    return lax.broadcasted_iota(jnp.int32, shape, dim)


def _rmsnorm_kernel(x_ref, g_ref, o_ref, *, eps):
    x = x_ref[...]
    y = x * lax.rsqrt(jnp.mean(x * x, axis=-1, keepdims=True) + eps)
    o_ref[...] = (y * g_ref[...]).astype(o_ref.dtype)


def _rmsnorm(x, g, eps):
    m, d = x.shape
    tm = _tile(m, 256)
    return pl.pallas_call(
        functools.partial(_rmsnorm_kernel, eps=eps),
        grid=(m // tm,),
        in_specs=[pl.BlockSpec((tm, d), lambda i: (i, 0)), pl.BlockSpec((1, d), lambda i: (0, 0))],
        out_specs=pl.BlockSpec((tm, d), lambda i: (i, 0)),
        out_shape=jax.ShapeDtypeStruct((m, d), BF16),
        compiler_params=_cparams(("parallel",)),
        name="rmsnorm",
    )(x, g.reshape(1, d))


def _norm_residual_kernel(f_ref, x_ref, gpost_ref, *rest, has_next):
    f = f_ref[...]
    y = f * lax.rsqrt(jnp.mean(f * f, axis=-1, keepdims=True) + NORM_EPS) * gpost_ref[...]
    xn = x_ref[...] + y
    if has_next:
        gnext_ref, xo_ref, ho_ref = rest
        xo_ref[...] = xn
        h = xn * lax.rsqrt(jnp.mean(xn * xn, axis=-1, keepdims=True) + NORM_EPS) * gnext_ref[...]
        ho_ref[...] = h.astype(ho_ref.dtype)
    else:
        (xo_ref,) = rest
        xo_ref[...] = xn


def _norm_residual(f, x, g_post, g_next):
    m, d = x.shape
    tm = _tile(m, 256)
    row = pl.BlockSpec((tm, d), lambda i: (i, 0))
    vec = pl.BlockSpec((1, d), lambda i: (0, 0))
    has_next = g_next is not None
    ins = [f, x, g_post.reshape(1, d)] + ([g_next.reshape(1, d)] if has_next else [])
    out = pl.pallas_call(
        functools.partial(_norm_residual_kernel, has_next=has_next),
        grid=(m // tm,),
        in_specs=[row, row, vec] + ([vec] if has_next else []),
        out_specs=[row, row] if has_next else [row],
        out_shape=([jax.ShapeDtypeStruct((m, d), F32), jax.ShapeDtypeStruct((m, d), BF16)]
                   if has_next else [jax.ShapeDtypeStruct((m, d), F32)]),
        compiler_params=_cparams(("parallel",)),
        name="norm_residual",
    )(*ins)
    return (out[0], out[1]) if has_next else (out[0], None)


def _dense_kernel(*refs, fn, n_rows, n_w, n_main_out, has_side):
    refs = list(refs)
    main_rows = refs[:n_rows]
    side_rows = refs[n_rows:2 * n_rows] if has_side else []
    pos = n_rows * (2 if has_side else 1)
    w_refs = refs[pos:pos + n_w]
    main_outs = refs[pos + n_w:pos + n_w + n_main_out]
    side_outs = refs[pos + n_w + n_main_out:]
    wv = [w[...].astype(BF16) for w in w_refs]
    fn(main_rows, wv, main_outs)
    if has_side:
        @pl.when(pl.program_id(0) == 0)
        def _():
            fn(side_rows, wv, side_outs)


def _dense(fn, row_ins, w_ins, out_defs, *, n, tn, tm=MM_TILE, k_steps=1, name):
    m = row_ins[0][0].shape[0]
    has_side = row_ins[0][1] is not None
    tm = _tile(m, tm)
    nj = n // tn
    w_mode = dict(pipeline_mode=pl.Buffered(1)) if nj == 1 and k_steps == 1 else {}
    in_specs, ins = [], []
    for which in ((0, 1) if has_side else (0,)):
        for entry in row_ins:
            a, k_tiled = entry[which], entry[2]
            rows = tm if which == 0 else a.shape[0]
            width = a.shape[1] // k_steps if k_tiled else a.shape[1]
            if which == 0:
                imap = (lambda i, j, kk: (i, kk)) if k_tiled else (lambda i, j, kk: (i, 0))
            else:
                imap = (lambda i, j, kk: (0, kk)) if k_tiled else (lambda i, j, kk: (0, 0))
            in_specs.append(pl.BlockSpec((rows, width), imap))
            ins.append(a)
    for w, layer, k_rows, rb, cb in w_ins:
        in_specs.append(pl.BlockSpec((None, k_rows, tn),
                                     lambda i, j, kk, layer=layer, rb=rb, cb=cb: (layer, rb + kk, cb + j), **w_mode))
        ins.append(w)
    out_specs, out_shape = [], []
    for dt, hw, _ in out_defs:
        if hw is None:
            out_specs.append(pl.BlockSpec((tm, tn), lambda i, j, kk: (i, j)))
            out_shape.append(jax.ShapeDtypeStruct((m, n), dt))
        else:
            out_specs.append(pl.BlockSpec((tm, tn // hw, hw), lambda i, j, kk: (i, j, 0)))
            out_shape.append(jax.ShapeDtypeStruct((m, n // hw, hw), dt))
    n_main_out = len(out_specs)
    if has_side:
        ms = row_ins[0][1].shape[0]
        for _, _, side_dt in out_defs:
            if side_dt is not None:
                out_specs.append(pl.BlockSpec((ms, tn), lambda i, j, kk: (0, jnp.where(i == 0, j, nj - 1))))
                out_shape.append(jax.ShapeDtypeStruct((ms, n), side_dt))
    out = pl.pallas_call(
        functools.partial(_dense_kernel, fn=fn, n_rows=len(row_ins), n_w=len(w_ins), n_main_out=n_main_out,
                          has_side=has_side),
        grid=(m // tm, nj, k_steps),
        in_specs=in_specs,
        out_specs=out_specs,
        out_shape=out_shape,
        compiler_params=_cparams(("arbitrary", "arbitrary", "arbitrary")),
        name=name,
    )(*ins)
    return out[:n_main_out], out[n_main_out:]


def _proj_fn(rows, wv, outs, *, rope, scale, hw):
    acc = jnp.dot(rows[0][...], wv[0], preferred_element_type=F32)
    if rope:
        cos = rows[1][...]
        sin = rows[2][...]
    for g in range(acc.shape[1] // hw):
        x = acc[:, g * hw:(g + 1) * hw]
        if rope:
            x = x * cos + pltpu.roll(x, hw // 2, 1) * sin
        if scale != 1.0:
            x = x * scale
        for o in outs:
            if len(o.shape) == 3:
                o[:, g, :] = x.astype(o.dtype)
            else:
                o[:, g * hw:(g + 1) * hw] = x.astype(o.dtype)


def _mm2_fn(rows, wv, outs):
    outs[0][...] = (jnp.dot(rows[0][...], wv[0], preferred_element_type=F32)
                    + jnp.dot(rows[1][...], wv[1], preferred_element_type=F32))


def _ffn_up_fn(rows, wv, outs):
    u = jnp.maximum(jnp.dot(rows[0][...], wv[0], preferred_element_type=F32), 0.0)
    outs[0][...] = (u * u).astype(outs[0].dtype)


def _mm_acc_fn(rows, wv, outs):
    part = jnp.dot(rows[0][...], wv[0], preferred_element_type=F32)

    @pl.when(pl.program_id(2) == 0)
    def _():
        outs[0][...] = part

    @pl.when(pl.program_id(2) != 0)
    def _():
        outs[0][...] += part


def _ple_fn(rows, wv, outs):
    gate = jnp.dot(rows[0][...], wv[0], preferred_element_type=F32)
    gate = 1.0 / (1.0 + jnp.exp(-gate))
    proj = jnp.dot(rows[1][...].astype(BF16), wv[1], preferred_element_type=F32)
    outs[0][...] = gate * proj


def _lambda(lq1_ref, lk1_ref, lq2_ref, lk2_ref, lam_init):
    s1 = jnp.sum(lq1_ref[...] * lk1_ref[...], axis=-1, keepdims=True)
    s2 = jnp.sum(lq2_ref[...] * lk2_ref[...], axis=-1, keepdims=True)
    return jnp.exp(s1) - jnp.exp(s2) + lam_init


def _subln(o, g, lam_init):
    y = o * lax.rsqrt(jnp.mean(o * o, axis=-1, keepdims=True) + SUBLN_EPS)
    return y * g * (1.0 - lam_init)


def _lane_tile(x, width):
    return jnp.concatenate([x] * (width // LANES), axis=1)


def _prompt_attn_kernel(it_ref, jt_ref, q_ref, k_ref, v_ref, lq1_ref, lk1_ref, lq2_ref, lk2_ref, g_ref,
                        o_ref, s_sc, p_sc, a_sc, m_sc, l_sc, acc_sc, *, lam_init, dk, strip):
    s = pl.program_id(1)
    i = it_ref[s]
    j = jt_ref[s]
    t = q_ref.shape[0]
    dv = v_ref.shape[1]

    @pl.when(j == 0)
    def _():
        m_sc[...] = jnp.full(m_sc.shape, NEG_INF, F32)
        l_sc[...] = jnp.zeros(l_sc.shape, F32)
        acc_sc[...] = jnp.zeros(acc_sc.shape, F32)

    def step(masked):
        for mp in range(2):
            s_sc[mp] = _dot_nt(q_ref[:, mp * dk:(mp + 1) * dk], k_ref[:, mp * dk:(mp + 1) * dk])

        v = v_ref[...]
        for mp in range(2):
            for r in range(t // strip):
                rows = slice(r * strip, (r + 1) * strip)
                sc = s_sc[mp, rows, :]
                if masked:
                    sc = jnp.where(_iota(sc.shape, 1) <= _iota(sc.shape, 0) + r * strip, sc, NEG_INF)
                m_prev = m_sc[mp, rows, :]
                m_new = jnp.maximum(m_prev, jnp.max(sc, axis=-1, keepdims=True))
                alpha = jnp.exp2(m_prev - m_new)
                p = jnp.exp2(sc - _lane_tile(m_new, t))
                l_sc[mp, rows, :] = alpha * l_sc[mp, rows, :] + jnp.sum(p, axis=-1, keepdims=True)
                a_sc[mp, rows, :] = alpha
                m_sc[mp, rows, :] = m_new
                p_sc[mp, rows, :] = p.astype(BF16)
            acc_sc[mp] = (_lane_tile(a_sc[mp], dv) * acc_sc[mp]
                          + jnp.dot(p_sc[mp], v, preferred_element_type=F32))

    @pl.when(j < i)
    def _():
        step(False)

    @pl.when(j == i)
    def _():
        step(True)
        lam = _lambda(lq1_ref, lk1_ref, lq2_ref, lk2_ref, lam_init)
        o = acc_sc[0] / _lane_tile(l_sc[0], dv) - lam * (acc_sc[1] / _lane_tile(l_sc[1], dv))
        o_ref[...] = _subln(o, g_ref[...], lam_init).astype(o_ref.dtype)


def _prompt_attention(q, k, v, lam_vecs, g, lam_init, heads, dk, dv):
    s_len = q.shape[0]
    t = _tile(s_len, ATTN_TILE)
    nb = s_len // t
    pairs = [(i, j) for i in range(nb) for j in range(i + 1)]
    it = jnp.asarray([p[0] for p in pairs], jnp.int32)
    jt = jnp.asarray([p[1] for p in pairs], jnp.int32)
    vec = pl.BlockSpec((1, dk), lambda h, s, it, jt: (0, 0))
    grid_spec = pltpu.PrefetchScalarGridSpec(
        num_scalar_prefetch=2,
        grid=(heads, len(pairs)),
        in_specs=[pl.BlockSpec((t, 2 * dk), lambda h, s, it, jt: (it[s], h)),
                  pl.BlockSpec((t, 2 * dk), lambda h, s, it, jt: (jt[s], h)),
                  pl.BlockSpec((t, dv), lambda h, s, it, jt: (jt[s], h)),
                  vec, vec, vec, vec,
                  pl.BlockSpec((1, dv), lambda h, s, it, jt: (0, 0))],
        out_specs=pl.BlockSpec((t, dv), lambda h, s, it, jt: (it[s], h)),
        scratch_shapes=[pltpu.VMEM((2, t, t), F32), pltpu.VMEM((2, t, t), BF16), pltpu.VMEM((2, t, LANES), F32),
                        pltpu.VMEM((2, t, LANES), F32), pltpu.VMEM((2, t, LANES), F32),
                        pltpu.VMEM((2, t, dv), F32)],
    )
    return pl.pallas_call(
        functools.partial(_prompt_attn_kernel, lam_init=lam_init, dk=dk, strip=min(ATTN_STRIP, t)),
        grid_spec=grid_spec,
        out_shape=jax.ShapeDtypeStruct((s_len, heads * dv), BF16),
        compiler_params=_cparams(("parallel", "arbitrary")),
        name="prompt_attention",
    )(it, jt, q, k, v, *lam_vecs, g.reshape(1, dv))


def _sample_attn_kernel(pt_ref, q_ref, kn_ref, vn_ref, *rest, lam_init, heads, dk, dv, pages):
    k_refs = rest[:pages]
    v_refs = rest[pages:2 * pages]
    (lq1_ref, lk1_ref, lq2_ref, lk2_ref, g_ref, o_ref, q_sc, bias_sc, m_sc, l_sc, acc_sc) = rest[2 * pages:]
    step_id = pl.program_id(1)
    half = SUBLANES // 2
    rp = heads * half
    page_lanes = bias_sc.shape[1]

    def own_head(shape):
        return _iota(shape, 1) % heads == (_iota(shape, 0) % rp) // half

    @pl.when(step_id == 0)
    def _():
        q = q_ref[0:SUBLANES, :]
        for par in range(2):
            for j in range(heads // 2):
                lo = q[:, (4 * j + par) * dk:(4 * j + par + 1) * dk]
                hi = q[:, (4 * j + 2 + par) * dk:(4 * j + 3 + par) * dk]
                q_sc[par * rp + j * SUBLANES:par * rp + (j + 1) * SUBLANES, :] = lo + pltpu.roll(hi, half, 0)
        bias_sc[...] = jnp.where(own_head(bias_sc.shape), 0.0, NEG_INF)
        m_sc[...] = jnp.full(m_sc.shape, NEG_INF, F32)
        l_sc[...] = jnp.zeros(l_sc.shape, F32)
        acc_sc[...] = jnp.zeros(acc_sc.shape, F32)

    def scores(k_ref, lanes):
        return jnp.concatenate(
            [_dot_nt(q_sc[par * rp:(par + 1) * rp, :], k_ref[pl.ds(par, lanes, stride=2), :]) for par in range(2)],
            axis=0)

    def update(sc, v_list):
        m_prev = m_sc[...]
        m_new = jnp.maximum(m_prev, jnp.max(sc, axis=-1, keepdims=True))
        alpha = jnp.exp2(m_prev - m_new)
        p = jnp.exp2(sc - m_new)
        l_sc[...] = alpha * l_sc[...] + jnp.sum(p, axis=-1, keepdims=True)
        pv, off = None, 0
        for vr in v_list:
            n = vr.shape[0]
            part = _dot(p[:, off:off + n], vr[...])
            pv = part if pv is None else pv + part
            off += n
        acc_sc[...] = alpha * acc_sc[...] + pv
        m_sc[...] = m_new

    bias = bias_sc[...]
    update(jnp.concatenate([scores(kr, page_lanes) + bias for kr in k_refs], axis=1), v_refs)

    @pl.when(step_id == pl.num_programs(1) - 1)
    def _():
        sc = scores(kn_ref, vn_ref.shape[0])
        ok = own_head(sc.shape) & (_iota(sc.shape, 1) // heads <= _iota(sc.shape, 0) % half)
        update(jnp.where(ok, sc, NEG_INF), [vn_ref])
        lam = _lambda(lq1_ref, lk1_ref, lq2_ref, lk2_ref, lam_init)
        accn = acc_sc[...] / l_sc[...]
        y = _subln(accn[:rp] - lam * accn[rp:], g_ref[...], lam_init)
        for j in range(heads // 2):
            y8 = y[j * SUBLANES:(j + 1) * SUBLANES]
            o_ref[:, (2 * j) * dv:(2 * j + 1) * dv] = y8
            o_ref[:, (2 * j + 1) * dv:(2 * j + 2) * dv] = pltpu.roll(y8, half, 0)


def _sample_attention(q, k_new, v_new, cache_k, cache_v, pool_offset, page_table, lam_vecs, g, lam_init, n_new):
    b, pad_t, width = q.shape
    n_pages = page_table.shape[1]
    dk, dv = cache_k.shape[2], cache_v.shape[2]
    maps = width // dk
    heads = maps // 2
    page = cache_k.shape[1] // maps
    half = SUBLANES // 2
    assert n_new <= half and pad_t >= SUBLANES and heads % 2 == 0
    pages = math.gcd(n_pages, PAGES_PER_STEP)
    rows = 2 * heads * half
    vec = pl.BlockSpec((1, dk), lambda bi, s, pt: (0, 0))
    per_row = lambda a: pl.BlockSpec((None,) + a.shape[1:], lambda bi, s, pt: (bi, 0, 0))

    def page_spec(pg, a):
        return pl.BlockSpec((None,) + a.shape[1:], lambda bi, s, pt: (pool_offset + pt[bi, s * pages + pg], 0, 0))

    grid_spec = pltpu.PrefetchScalarGridSpec(
        num_scalar_prefetch=1,
        grid=(b, n_pages // pages),
        in_specs=([per_row(q), per_row(k_new), per_row(v_new)] + [page_spec(pg, cache_k) for pg in range(pages)]
                  + [page_spec(pg, cache_v) for pg in range(pages)]
                  + [vec, vec, vec, vec, pl.BlockSpec((1, dv), lambda bi, s, pt: (0, 0))]),
        out_specs=pl.BlockSpec((None, SUBLANES, heads * dv), lambda bi, s, pt: (bi, 0, 0)),
        scratch_shapes=[pltpu.VMEM((rows, dk), F32), pltpu.VMEM((rows, page * heads), F32),
                        pltpu.VMEM((rows, 1), F32), pltpu.VMEM((rows, 1), F32), pltpu.VMEM((rows, dv), F32)],
    )
    return pl.pallas_call(
        functools.partial(_sample_attn_kernel, lam_init=lam_init, heads=heads, dk=dk, dv=dv, pages=pages),
        grid_spec=grid_spec,
        out_shape=jax.ShapeDtypeStruct((b, SUBLANES, heads * dv), F32),
        compiler_params=_cparams(("parallel", "arbitrary")),
        name="sample_attention",
    )(page_table, q, k_new, v_new, *([cache_k] * pages), *([cache_v] * pages), *lam_vecs, g.reshape(1, dv))


def _segsums(xs, ones_bd):
    parts = []
    for x in xs:
        hi = x.astype(BF16)
        parts += [hi, (x - hi.astype(F32)).astype(BF16)]
    s = jnp.dot(jnp.concatenate(parts, axis=0), ones_bd, preferred_element_type=F32)
    m = xs[0].shape[0]
    return [s[2 * i * m:(2 * i + 1) * m] + s[(2 * i + 1) * m:(2 * i + 2) * m] for i in range(len(xs))]


def _segsum(x, ones_bd):
    return _segsums([x], ones_bd)[0]


def _block_ones(w):
    return jnp.where(_iota((w, w), 0) // RW_N == _iota((w, w), 1) // RW_N, 1.0, 0.0).astype(BF16)


def _rw_prep_kernel(z_ref, zprev_ref, mu_ref, w0_ref, w2_ref, a0_ref, a2_ref, g2_ref, kk_ref, ka_ref,
                    r_out, lw_out, k_out, v_out, kk_out, b_out, g_out, carry_sc, *, mode, period, rw, dd, da, gp):
    z = z_ref[...]
    tm = z.shape[0]
    rolled = pltpu.roll(z, 1, 0)
    row = _iota(z.shape, 0)
    if mode == "carry":
        @pl.when(pl.program_id(0) == 0)
        def _():
            carry_sc[...] = zprev_ref[...]
        zs = jnp.where(row == 0, carry_sc[...], rolled)
        carry_sc[...] = z[tm - 1:tm, :]
    else:
        zs = jnp.where(row % period == 0, zprev_ref[...], rolled)
    zz = z + (zs - z) * mu_ref[...]
    r = zz[:, :rw]
    kr = zz[:, rw:2 * rw]
    v = zz[:, 2 * rw:3 * rw]
    zw = zz[:, 3 * rw:3 * rw + dd]
    za = zz[:, 3 * rw + dd:3 * rw + dd + da]
    zg = zz[:, 3 * rw + dd + da:3 * rw + dd + da + gp]
    u = w0_ref[...] + _dot(jnp.tanh(zw), w2_ref[...])
    w_log = jnp.minimum(u, 0.0) - jnp.log(1.0 + jnp.exp(-jnp.abs(u))) - 0.5
    lw_out[...] = -jnp.exp(w_log)
    a = 1.0 / (1.0 + jnp.exp(-(a0_ref[...] + _dot(za, a2_ref[...]))))
    g_out[...] = _dot(1.0 / (1.0 + jnp.exp(-zg)), g2_ref[...])
    ones_bd = _block_ones(LANES)
    kk = kr * kk_ref[...]
    for c in range(rw // LANES):
        sl = slice(c * LANES, (c + 1) * LANES)
        kc = kk[:, sl]
        kc = kc * lax.rsqrt(jnp.maximum(_segsum(kc * kc, ones_bd), 1e-24))
        kk_out[:, sl] = kc
        b_out[:, sl] = kc * a[:, sl]
    r_out[...] = r
    v_out[...] = v
    k_out[...] = kr * (1.0 + (a - 1.0) * ka_ref[...])


def _rw_prep(z, zprev, prm, mode, period):
    m, zp = z.shape
    rw, dd, da, gp = prm["rw"], prm["dd"], prm["da"], prm["gp"]
    tm = _tile(m, 128)
    row = pl.BlockSpec((tm, zp), lambda i: (i, 0))
    full = lambda a: pl.BlockSpec(a.shape, lambda i: (0,) * a.ndim)
    zprev_spec = full(zprev) if mode == "carry" else row
    params = [prm["mu"], prm["w0"], prm["w2"], prm["a0"], prm["a2"], prm["g2"], prm["k_k"], prm["k_a"]]
    out_spec = pl.BlockSpec((tm, rw), lambda i: (i, 0))
    return pl.pallas_call(
        functools.partial(_rw_prep_kernel, mode=mode, period=period, rw=rw, dd=dd, da=da, gp=gp),
        grid=(m // tm,),
        in_specs=[row, zprev_spec] + [full(p) for p in params],
        out_specs=[out_spec] * 7,
        out_shape=[jax.ShapeDtypeStruct((m, rw), F32)] * 7,
        scratch_shapes=[pltpu.VMEM((1, zp), F32)],
        compiler_params=_cparams(("arbitrary",)),
        name="rwkv_prep",
    )(z, zprev, *params)


def _bd(x, gw):
    head = (_iota(x.shape, 1) % gw) // RW_N
    return jnp.concatenate([jnp.where(head == h, x, 0.0) for h in range(gw // RW_N)], axis=0)


def _rw_scan_kernel(*refs, has_s0, t_valid, gw, n_groups):
    if has_s0:
        (r_ref, lw_ref, k_ref, v_ref, kk_ref, b_ref, g_ref, rk_ref, lng_ref, lnb_ref, s0_ref,
         y_out, s_out, sb_sc) = refs
    else:
        (r_ref, lw_ref, k_ref, v_ref, kk_ref, b_ref, g_ref, rk_ref, lng_ref, lnb_ref,
         y_out, s_out, sb_sc) = refs
    c = RW_CHUNK
    n = RW_N
    tc = r_ref.shape[0]
    ci = pl.program_id(1)
    last = pl.num_programs(1) - 1

    tri_inc = jnp.where(_iota((c, c), 1) <= _iota((c, c), 0), 1.0, 0.0).astype(BF16)
    s_idx = _iota((c, gw), 1) % n
    t_idx = _iota((c, gw), 0)
    strict = s_idx < t_idx
    incl = s_idx <= t_idx
    eye_w = jnp.where(s_idx == t_idx, 1.0, 0.0)
    gi0, gi1 = _iota((gw, gw), 0), _iota((gw, gw), 1)
    blk = gi0 // n == gi1 // n
    eye_g = jnp.where(gi0 == gi1, 1.0, 0.0)
    ones_bd = jnp.where(blk, 1.0, 0.0).astype(BF16)
    sel = jnp.where(_iota((gw, n), 0) % n == _iota((gw, n), 1), 1.0, 0.0).astype(BF16)
    sel_t = jnp.where(_iota((n, gw), 1) % n == _iota((n, gw), 0), 1.0, 0.0).astype(BF16)
    valid = _iota((tc, gw), 0) < t_valid

    def load(ref, sl, fill=0.0):
        x = ref[:, sl]
        if t_valid < tc:
            x = jnp.where(valid, x, fill)
        if tc < c:
            x = jnp.concatenate([x, jnp.full((c - tc, gw), fill, F32)], axis=0)
        return x

    @pl.when(ci == 0)
    def _():
        for gi in range(n_groups):
            if has_s0:
                sb_sc[gi] = jnp.where(blk, _dot_sel_r(s0_ref[gi * gw:(gi + 1) * gw, :], sel_t), 0.0)
            else:
                sb_sc[gi] = jnp.zeros((gw, gw), F32)

    def group_chain(gi):
        sl = slice(gi * gw, (gi + 1) * gw)
        r, lw, k, v, kk, b = (load(ref, sl) for ref in (r_ref, lw_ref, k_ref, v_ref, kk_ref, b_ref))
        cl = _dot_sel(tri_inc, lw)
        yield
        cend = cl[c - 1:c, :]
        kt = kk * jnp.exp(cl - lw)
        rt = r * jnp.exp(cl)
        g_inv = jnp.exp(-cl)
        g_rat = jnp.exp(cend - cl)
        ktil, btil = k * g_inv, b * g_inv
        khat, bhat = k * g_rat, b * g_rat
        la = _dot_nt(jnp.concatenate([kt, rt], axis=0),
                     jnp.concatenate([_bd(btil, gw), _bd(ktil, gw)], axis=0))
        yield
        lb = jnp.where(strict, la[:c, :gw], 0.0)
        lk = jnp.where(strict, la[:c, gw:], 0.0)
        ab = jnp.where(incl, la[c:, :gw], 0.0)
        ak = jnp.where(incl, la[c:, gw:], 0.0)
        lav = _dot(jnp.concatenate([lk, ak], axis=0), _bd(v, gw))
        lkv, y0 = lav[:c], lav[c:]
        qt = _dot_tn(v, khat)
        x = eye_w - lb
        rounds = max(0, math.ceil(math.log2(t_valid)) - 1)
        if rounds:
            p = _dot(lb, _bd(lb, gw))
            yield
        for rnd in range(rounds):
            if rnd == rounds - 1:
                x = x + _dot(x, _bd(p, gw))
            else:
                px = _dot(jnp.concatenate([p, x], axis=0), _bd(p, gw))
                p, x = px[:c], x + px[c:]
            yield
        tw = _dot(x, _bd(jnp.concatenate([kt, lkv], axis=1), gw))
        yield
        abw = _dot(ab, _bd(tw, gw))
        tn = _dot_tn(tw, bhat)
        yield
        rr = rt - abw[:, :gw]
        y0 = y0 - abw[:, gw:]
        pt = eye_g * jnp.exp(cend) - jnp.where(blk, tn[:gw], 0.0)
        qt = jnp.where(blk, qt - tn[gw:], 0.0)
        s_old = sb_sc[gi]
        y = _dot_nt(rr, s_old) + y0
        sb_sc[gi] = _dot(s_old, pt) + qt
        yield
        y_sum, rk_sum = _segsums([y, r * k * rk_ref[:, sl]], ones_bd)
        bonus = rk_sum * v
        yield
        d = y - y_sum * (1.0 / n)
        var = _segsum(d * d, ones_bd) * (1.0 / n)
        yield
        yn = d * lax.rsqrt(var + RW_LN_EPS) * lng_ref[:, sl] + lnb_ref[:, sl]
        out = (yn + bonus)[:tc] * g_ref[:, sl]
        y_out[:, sl] = out.astype(y_out.dtype)

    chains = [group_chain(gi) for gi in range(n_groups)]
    while chains:
        chains = [ch for ch in chains if next(ch, "done") != "done"]

    @pl.when(ci == last)
    def _():
        for gi in range(n_groups):
            s_out[gi * gw:(gi + 1) * gw, :] = _dot_sel_r(sb_sc[gi], sel)


def _rw_scan(r, lw, k, v, kk, b, g, prm, s0, n_seq, tc, t_valid):
    m, rw = r.shape
    n_chunks = m // (n_seq * tc)
    gw = math.gcd(rw, RW_GROUP)
    n_groups = rw // gw
    row = pl.BlockSpec((tc, rw), lambda s, ci: (s * n_chunks + ci, 0))
    vec = pl.BlockSpec((1, rw), lambda s, ci: (0, 0))
    state = pl.BlockSpec((None, rw, RW_N), lambda s, ci: (s, 0, 0))
    has_s0 = s0 is not None
    ins = [r, lw, k, v, kk, b, g, prm["r_k"], prm["ln_g"], prm["ln_b"]] + ([s0] if has_s0 else [])
    return pl.pallas_call(
        functools.partial(_rw_scan_kernel, has_s0=has_s0, t_valid=t_valid, gw=gw, n_groups=n_groups),
        grid=(n_seq, n_chunks),
        in_specs=[row] * 7 + [vec] * 3 + ([state] if has_s0 else []),
        out_specs=[row, state],
        out_shape=[jax.ShapeDtypeStruct((m, rw), BF16), jax.ShapeDtypeStruct((n_seq, rw, RW_N), F32)],
        scratch_shapes=[pltpu.VMEM((n_groups, gw, gw), F32)],
        compiler_params=_cparams(("parallel", "arbitrary")),
        name="rwkv_scan",
    )(*ins)


def _rope_tables(pos, dk):
    half = dk // 2
    inv = ROPE_THETA ** (-jnp.arange(half, dtype=F32) / half)
    ang = pos.astype(F32)[:, None] * inv[None, :]
    cos, sin = jnp.cos(ang), jnp.sin(ang)
    return jnp.concatenate([cos, cos], axis=-1), jnp.concatenate([-sin, sin], axis=-1)


def _prep_weights(i, dims, w_in, rw_mu, rw_w0, rw_w2, rw_a0, rw_a2, rw_g2, rw_k_k, rw_k_a, rw_r_k, rw_ln_g,
                  rw_ln_b):
    qw, da_w, rw, dd, da, dg, gp, zp = (dims[n] for n in ("qw", "da_w", "rw", "dd", "da", "dg", "gp", "zp"))
    rw_cols = 3 * rw + dd + da + dg
    w = jnp.pad(w_in[i], ((0, 0), (0, zp - rw_cols))).astype(BF16)[None]
    prm = {
        "rw": rw, "dd": dd, "da": da, "gp": gp,
        "mu": jnp.pad(rw_mu[i], (0, zp - rw_cols)).reshape(1, zp),
        "w0": rw_w0[i].reshape(1, rw), "w2": rw_w2[i].astype(BF16),
        "a0": rw_a0[i].reshape(1, rw), "a2": rw_a2[i].astype(BF16),
        "g2": jnp.pad(rw_g2[i], ((0, gp - dg), (0, 0))).astype(BF16),
        "k_k": rw_k_k[i].reshape(1, rw), "k_a": rw_k_a[i].reshape(1, rw),
        "r_k": rw_r_k[i].reshape(1, rw), "ln_g": rw_ln_g[i].reshape(1, rw), "ln_b": rw_ln_b[i].reshape(1, rw),
    }
    return w, prm


def _tail(i, xs, os_, rwos, pes, w_out, w_up, w_down, w_ple_gate, w_ple_proj, gains):
    g_post_mix, g_pre_ffn, g_post_ffn, g_pre_ple, g_post_ple = gains
    d = xs[0].shape[1]
    d_ff = w_up.shape[2]
    half = os_[0].shape[1]
    assert rwos[0].shape[1] == half and w_out.shape[1] == 2 * half
    tn = _tile(d, MM_TILE)
    mix = _dense(_mm2_fn, [(os_[0], os_[1], False), (rwos[0], rwos[1], False)],
                 [(w_out, i, half, 0, 0), (w_out, i, half, 1, 0)], [(F32, None, F32)], n=d, tn=tn,
                 tm=MM_TILE // 2, name="out_proj")
    x, h = zip(*(_norm_residual(f[0], xx, g_post_mix, g_pre_ffn) for f, xx in zip(mix, xs)))
    tk = _tile(d_ff, FFN_TK)
    f = []
    for hh in h:
        (u,), _ = _dense(_ffn_up_fn, [(hh, None, False)], [(w_up, i, d, 0, 0)], [(BF16, None, None)],
                         n=d_ff, tn=_tile(d_ff, MM_TILE), name="ffn_up")
        f.append(_dense(_mm_acc_fn, [(u, None, True)], [(w_down, i, tk, 0, 0)], [(F32, None, None)],
                        n=d, tn=tn, k_steps=d_ff // tk, name="ffn_down")[0])
    x, h = zip(*(_norm_residual(ff[0], xx, g_post_ffn, g_pre_ple) for ff, xx in zip(f, x)))
    e = _dense(_ple_fn, [(h[0], h[1], False), (pes[0], pes[1], False)],
               [(w_ple_gate, i, d, 0, 0), (w_ple_proj, i, pes[0].shape[1], 0, 0)], [(F32, None, F32)],
               n=d, tn=tn, tm=MM_TILE // 2, name="ple")
    return tuple(_norm_residual(ee[0], xx, g_post_ple, None)[0] for ee, xx in zip(e, x))


def kernel(x_prompt, x_sample, cache_k, cache_v, state_wkv, state_shift, page_table, p_prompt, p_sample, g_pre_mix, w_in, lam_q1, lam_k1, lam_q2, lam_k2, da_subln_g, rw_mu, rw_w0, rw_w2, rw_a0, rw_a2, rw_g2, rw_k_k, rw_k_a, rw_r_k, rw_ln_g, rw_ln_b, w_out, g_post_mix, g_pre_ffn, w_up, w_down, g_post_ffn, g_pre_ple, w_ple_gate, w_ple_proj, g_post_ple):
    batch, seq, d = x_prompt.shape
    db, dseq, _ = x_sample.shape
    depth, n_pool, page, maps, dk = cache_k.shape
    heads, dv = cache_v.shape[3], cache_v.shape[4]
    rwh, rwn = rw_r_k.shape[1], rw_r_k.shape[2]
    assert maps == 2 * heads and dk == LANES and rwn == RW_N and seq % RW_CHUNK == 0
    qw, da_w, rw = maps * dk, heads * dv, rwh * rwn
    dd, da, dg = rw_w2.shape[1], rw_a2.shape[1], rw_g2.shape[1]
    assert dd % LANES == 0 and da % LANES == 0 and rw % LANES == 0
    gp = _round_up(dg, LANES)
    rw_cols = 3 * rw + dd + da + dg
    zp = _round_up(3 * rw + dd + da + gp, min(MM_TILE, _round_up(rw_cols, LANES)))
    dims = dict(qw=qw, da_w=da_w, rw=rw, dd=dd, da=da, dg=dg, gp=gp, zp=zp)
    n_pages = page_table.shape[1]
    past_len = n_pages * page
    pad_t = _round_up(dseq, SUBLANES)
    pad_a = _round_up(dseq, 2 * SUBLANES)
    scale = dk ** -0.5

    cos_p, sin_p = _rope_tables(jnp.tile(jnp.arange(seq, dtype=jnp.int32), batch), dk)
    cos_s, sin_s = _rope_tables(jnp.tile(past_len + jnp.arange(dseq, dtype=jnp.int32), db), dk)

    xp = x_prompt.reshape(batch * seq, d)
    xs = x_sample.reshape(db * dseq, d)
    tail_w = [a.astype(BF16) for a in (w_out, w_up, w_down, w_ple_gate, w_ple_proj)]
    outs = [[] for _ in range(8)]
    for i in range(depth):
        lam_init = 0.8 - 0.6 * math.exp(-0.3 * i)
        w, prm = _prep_weights(i, dims, w_in, rw_mu, rw_w0, rw_w2, rw_a0, rw_a2, rw_g2, rw_k_k, rw_k_a, rw_r_k,
                               rw_ln_g, rw_ln_b)
        lam_vecs = [v[i].reshape(1, dk) for v in (lam_q1, lam_k1, lam_q2, lam_k2)]
        gains = (g_post_mix[i], g_pre_ffn[i], g_post_ffn[i], g_pre_ple[i], g_post_ple[i])

        h = _rmsnorm(xp, g_pre_mix[i], NORM_EPS)
        h_s = _rmsnorm(xs, g_pre_mix[i], NORM_EPS)
        roped = [(h, h_s, False), (cos_p, cos_s, False), (sin_p, sin_s, False)]
        tn_q, tn_z = _tile(qw, MM_TILE), _tile(zp, MM_TILE)
        assert qw % tn_q == 0 and (2 * qw) % da_w == 0 and (2 * qw + da_w) % tn_z == 0
        (q,), (q_s,) = _dense(
            functools.partial(_proj_fn, rope=True, scale=scale * math.log2(math.e), hw=dk),
            roped, [(w, 0, d, 0, 0)], [(BF16, None, F32)], n=qw, tn=tn_q, name="proj_q")
        (k32, k16), (k32_s,) = _dense(
            functools.partial(_proj_fn, rope=True, scale=1.0, hw=dk),
            roped, [(w, 0, d, 0, qw // tn_q)], [(F32, dk, F32), (BF16, None, None)], n=qw, tn=tn_q,
            tm=MM_TILE // 2, name="proj_k")
        v_fn = functools.partial(_proj_fn, rope=False, scale=1.0, hw=dv)
        v_w = [(w, 0, d, 0, 2 * qw // da_w)]
        (v32, v16), _ = _dense(v_fn, [(h, None, False)], v_w, [(F32, dv, None), (BF16, None, None)], n=da_w,
                               tn=da_w, tm=MM_TILE // 2, name="proj_v")
        (v32_s,), _ = _dense(v_fn, [(h_s, None, False)], v_w, [(F32, None, None)], n=da_w, tn=da_w, name="proj_v")
        (z,), (z_s,) = _dense(functools.partial(_proj_fn, rope=False, scale=1.0, hw=tn_z), [(h, h_s, False)],
                              [(w, 0, d, 0, (2 * qw + da_w) // tn_z)], [(F32, None, F32)], n=zp, tn=tn_z,
                              name="proj_z")

        o_rows, rw_rows, wkv_rows = [], [], []
        for bi in range(batch):
            rs = slice(bi * seq, (bi + 1) * seq)
            o_rows.append(_prompt_attention(q[rs], k16[rs], v16[rs], lam_vecs, da_subln_g[i], lam_init,
                                            heads, dk, dv))
            mixer_in = _rw_prep(z[rs], jnp.zeros((1, zp), F32), prm, "carry", 0)
            rwo, s_fin = _rw_scan(*mixer_in, prm, None, 1, RW_CHUNK, RW_CHUNK)
            rw_rows.append(rwo)
            wkv_rows.append(s_fin.reshape(rwh, rwn, rwn))
        o = o_rows[0] if batch == 1 else jnp.concatenate(o_rows, axis=0)
        rwo = rw_rows[0] if batch == 1 else jnp.concatenate(rw_rows, axis=0)
        outs[0].append(k32.reshape(batch, seq, maps, dk))
        outs[1].append(v32.reshape(batch, seq, heads, dv))
        outs[4].append(jnp.stack(wkv_rows, 0))
        outs[6].append(z.reshape(batch, seq, zp)[:, -1, :rw_cols])

        pad3 = lambda a, t: jnp.pad(a.reshape(db, dseq, -1), ((0, 0), (0, t - dseq), (0, 0)))
        o_s = _sample_attention(pad3(q_s, pad_a), pad3(k32_s, pad_a).reshape(db, pad_a * maps, dk),
                                pad3(v32_s, pad_a).reshape(db, pad_a * heads, dv),
                                cache_k.reshape(depth * n_pool, page * maps, dk),
                                cache_v.reshape(depth * n_pool, page * heads, dv), i * n_pool, page_table,
                                lam_vecs, da_subln_g[i], lam_init, dseq)
        o_s = o_s[:, :dseq].reshape(db * dseq, da_w).astype(BF16)
        zprev = jnp.zeros((db, pad_t, zp), F32).at[:, 0, :rw_cols].set(state_shift[i])
        mixer_in = _rw_prep(pad3(z_s, pad_t).reshape(db * pad_t, zp), zprev.reshape(db * pad_t, zp), prm, "rows",
                            pad_t)
        rwo_s, s_fin = _rw_scan(*mixer_in, prm, state_wkv[i].reshape(db, rw, rwn), db, pad_t, dseq)
        rwo_s = rwo_s.reshape(db, pad_t, rw)[:, :dseq].reshape(db * dseq, rw)
        outs[2].append(k32_s.reshape(db, dseq, maps, dk))
        outs[3].append(v32_s.reshape(db, dseq, heads, dv))
        outs[5].append(s_fin.reshape(db, rwh, rwn, rwn))
        outs[7].append(z_s.reshape(db, dseq, zp)[:, -1, :rw_cols])

        xp, xs = _tail(i, (xp, xs), (o, o_s), (rwo, rwo_s),
                       (p_prompt[i].reshape(batch * seq, -1), p_sample[i].reshape(db * dseq, -1)),
                       *tail_w, gains)

    st =[jnp.stack(o, 0) for o in outs]
    return (xp.reshape(batch, seq, d), xs.reshape(db, dseq, d), st[0], st[1], st[2], st[3], st[4], st[5],
            st[6], st[7])
```

```python
import functools
import math

import jax
import jax.numpy as jnp
from jax import lax
from jax.experimental import pallas as pl
from jax.experimental.pallas import tpu as pltpu

F32 = jnp.float32
BF16 = jnp.bfloat16

LANES = 128
SUBLANES = 8
VMEM_LIMIT_BYTES = 56 * 1024 * 1024

ROPE_THETA = 10000.0
NORM_EPS = 1e-6
SUBLN_EPS = 1e-5
RW_LN_EPS = 64e-5
NEG_INF = -1e30

RW_N = 64
RW_CHUNK = 64
RW_GROUP = 256
MM_TILE = 1024
FFN_TK = 4096
ATTN_TILE = 1024
ATTN_STRIP = 32
PAGES_PER_STEP = 8


def _cparams(sem):
    return pltpu.CompilerParams(dimension_semantics=sem, vmem_limit_bytes=VMEM_LIMIT_BYTES)


def _round_up(x, m):
    return (x + m - 1) // m * m


def _tile(n, pref):
    t = min(n, pref)
    assert n % t == 0, (n, t)
    return t


def _dot(a, b):
    return jnp.dot(a.astype(BF16), b.astype(BF16), preferred_element_type=F32)


def _dot_nt(a, b):
    return lax.dot_general(a.astype(BF16), b.astype(BF16), (((1,), (1,)), ((), ())),
                           preferred_element_type=F32)


def _dot_tn(a, b):
    return lax.dot_general(a.astype(BF16), b.astype(BF16), (((0,), (0,)), ((), ())),
                           preferred_element_type=F32)


def _split3(x):
    hi = x.astype(BF16)
    r1 = x - hi.astype(F32)
    mid = r1.astype(BF16)
    lo = (r1 - mid.astype(F32)).astype(BF16)
    return hi, mid, lo


def _dot_sel(sel, x):
    hi, mid, lo = _split3(x)
    n = x.shape[1]
    y = jnp.dot(sel, jnp.concatenate([hi, mid, lo], axis=1), preferred_element_type=F32)
    return y[:, :n] + y[:, n:2 * n] + y[:, 2 * n:]


def _dot_sel_r(x, sel):
    hi, mid, lo = _split3(x)
    m = x.shape[0]
    y = jnp.dot(jnp.concatenate([hi, mid, lo], axis=0), sel, preferred_element_type=F32)
    return y[:m] + y[m:2 * m] + y[2 * m:]


def _iota(shape, dim):
    return lax.broadcasted_iota(jnp.int32, shape, dim)


def _rmsnorm_kernel(x_ref, g_ref, o_ref, *, eps):
    x = x_ref[...]
    y = x * lax.rsqrt(jnp.mean(x * x, axis=-1, keepdims=True) + eps)
    o_ref[...] = (y * g_ref[...]).astype(o_ref.dtype)


def _rmsnorm(x, g, eps):
    m, d = x.shape
    tm = _tile(m, 256)
    return pl.pallas_call(
        functools.partial(_rmsnorm_kernel, eps=eps),
        grid=(m // tm,),
        in_specs=[pl.BlockSpec((tm, d), lambda i: (i, 0)), pl.BlockSpec((1, d), lambda i: (0, 0))],
        out_specs=pl.BlockSpec((tm, d), lambda i: (i, 0)),
        out_shape=jax.ShapeDtypeStruct((m, d), BF16),
        compiler_params=_cparams(("parallel",)),
        name="rmsnorm",
    )(x, g.reshape(1, d))


def _norm_residual_kernel(f_ref, x_ref, gpost_ref, *rest, has_next):
    f = f_ref[...]
    y = f * lax.rsqrt(jnp.mean(f * f, axis=-1, keepdims=True) + NORM_EPS) * gpost_ref[...]
    xn = x_ref[...] + y
    if has_next:
        gnext_ref, xo_ref, ho_ref = rest
        xo_ref[...] = xn
        h = xn * lax.rsqrt(jnp.mean(xn * xn, axis=-1, keepdims=True) + NORM_EPS) * gnext_ref[...]
        ho_ref[...] = h.astype(ho_ref.dtype)
    else:
        (xo_ref,) = rest
        xo_ref[...] = xn


def _norm_residual(f, x, g_post, g_next):
    m, d = x.shape
    tm = _tile(m, 256)
    row = pl.BlockSpec((tm, d), lambda i: (i, 0))
    vec = pl.BlockSpec((1, d), lambda i: (0, 0))
    has_next = g_next is not None
    ins = [f, x, g_post.reshape(1, d)] + ([g_next.reshape(1, d)] if has_next else [])
    out = pl.pallas_call(
        functools.partial(_norm_residual_kernel, has_next=has_next),
        grid=(m // tm,),
        in_specs=[row, row, vec] + ([vec] if has_next else []),
        out_specs=[row, row] if has_next else [row],
        out_shape=([jax.ShapeDtypeStruct((m, d), F32), jax.ShapeDtypeStruct((m, d), BF16)]
                   if has_next else [jax.ShapeDtypeStruct((m, d), F32)]),
        compiler_params=_cparams(("parallel",)),
        name="norm_residual",
    )(*ins)
    return (out[0], out[1]) if has_next else (out[0], None)


def _dense_kernel(*refs, fn, n_rows, n_w, n_main_out, has_side, has_cast):
    refs = list(refs)
    if has_cast:
        cast_in, cast_out = refs.pop(n_rows * (2 if has_side else 1) + n_w), refs.pop()
        cast_out[...] = cast_in[...].astype(cast_out.dtype)
    main_rows = refs[:n_rows]
    side_rows = refs[n_rows:2 * n_rows] if has_side else []
    pos = n_rows * (2 if has_side else 1)
    w_refs = refs[pos:pos + n_w]
    main_outs = refs[pos + n_w:pos + n_w + n_main_out]
    side_outs = refs[pos + n_w + n_main_out:]
    wv = [w[...].astype(BF16) for w in w_refs]
    fn(main_rows, wv, main_outs)
    if has_side:
        @pl.when(pl.program_id(0) == 0)
        def _():
            fn(side_rows, wv, side_outs)


def _dense(fn, row_ins, w_ins, out_defs, *, n, tn, tm=MM_TILE, k_steps=1, cast_src=None, name):
    m = row_ins[0][0].shape[0]
    has_side = row_ins[0][1] is not None
    tm = _tile(m, tm)
    nj = n // tn
    w_mode = dict(pipeline_mode=pl.Buffered(1)) if nj == 1 and k_steps == 1 else {}
    in_specs, ins = [], []
    for which in ((0, 1) if has_side else (0,)):
        for entry in row_ins:
            a, k_tiled = entry[which], entry[2]
            rows = tm if which == 0 else a.shape[0]
            width = a.shape[1] // k_steps if k_tiled else a.shape[1]
            if which == 0:
                imap = (lambda i, j, kk: (i, kk)) if k_tiled else (lambda i, j, kk: (i, 0))
            else:
                imap = (lambda i, j, kk: (0, kk)) if k_tiled else (lambda i, j, kk: (0, 0))
            in_specs.append(pl.BlockSpec((rows, width), imap))
            ins.append(a)
    for w, layer, k_rows, rb, cb in w_ins:
        in_specs.append(pl.BlockSpec((None, k_rows, tn),
                                     lambda i, j, kk, layer=layer, rb=rb, cb=cb: (layer, rb + kk, cb + j), **w_mode))
        ins.append(w)
    out_specs, out_shape = [], []
    for dt, hw, _ in out_defs:
        if hw is None:
            out_specs.append(pl.BlockSpec((tm, tn), lambda i, j, kk: (i, j)))
            out_shape.append(jax.ShapeDtypeStruct((m, n), dt))
        else:
            out_specs.append(pl.BlockSpec((tm, tn // hw, hw), lambda i, j, kk: (i, j, 0)))
            out_shape.append(jax.ShapeDtypeStruct((m, n // hw, hw), dt))
    n_main_out = len(out_specs)
    if has_side:
        ms = row_ins[0][1].shape[0]
        for _, _, side_dt in out_defs:
            if side_dt is not None:
                out_specs.append(pl.BlockSpec((ms, tn), lambda i, j, kk: (0, jnp.where(i == 0, j, nj - 1))))
                out_shape.append(jax.ShapeDtypeStruct((ms, n), side_dt))
    if cast_src is not None:
        src, layer = cast_src
        steps = (m // tm) * nj
        assert k_steps == 1 and src.shape[1] % steps == 0
        slab = src.shape[1] // steps
        in_specs.append(pl.BlockSpec((None, slab, src.shape[2]),
                                     lambda i, j, kk, layer=layer: (layer, i * nj + j, 0)))
        ins.append(src)
        out_specs.append(pl.BlockSpec((slab, src.shape[2]), lambda i, j, kk: (i * nj + j, 0)))
        out_shape.append(jax.ShapeDtypeStruct(src.shape[1:], BF16))
    out = pl.pallas_call(
        functools.partial(_dense_kernel, fn=fn, n_rows=len(row_ins), n_w=len(w_ins), n_main_out=n_main_out,
                          has_side=has_side, has_cast=cast_src is not None),
        grid=(m // tm, nj, k_steps),
        in_specs=in_specs,
        out_specs=out_specs,
        out_shape=out_shape,
        compiler_params=_cparams(("arbitrary", "arbitrary", "arbitrary")),
        name=name,
    )(*ins)
    if cast_src is not None:
        return out[:n_main_out], out[n_main_out:-1], out[-1][None]
    return out[:n_main_out], out[n_main_out:]


def _proj_fn(rows, wv, outs, *, rope, scale, hw):
    acc = jnp.dot(rows[0][...], wv[0], preferred_element_type=F32)
    if rope:
        cos = rows[1][...]
        sin = rows[2][...]
    for g in range(acc.shape[1] // hw):
        x = acc[:, g * hw:(g + 1) * hw]
        if rope:
            x = x * cos + pltpu.roll(x, hw // 2, 1) * sin
        if scale != 1.0:
            x = x * scale
        for o in outs:
            if len(o.shape) == 3:
                o[:, g, :] = x.astype(o.dtype)
            else:
                o[:, g * hw:(g + 1) * hw] = x.astype(o.dtype)


def _mm2_fn(rows, wv, outs):
    outs[0][...] = (jnp.dot(rows[0][...], wv[0], preferred_element_type=F32)
                    + jnp.dot(rows[1][...], wv[1], preferred_element_type=F32))


def _ffn_up_fn(rows, wv, outs):
    u = jnp.maximum(jnp.dot(rows[0][...], wv[0], preferred_element_type=F32), 0.0)
    outs[0][...] = (u * u).astype(outs[0].dtype)


def _mm_acc_fn(rows, wv, outs):
    part = jnp.dot(rows[0][...], wv[0], preferred_element_type=F32)

    @pl.when(pl.program_id(2) == 0)
    def _():
        outs[0][...] = part

    @pl.when(pl.program_id(2) != 0)
    def _():
        outs[0][...] += part


def _ple_fn(rows, wv, outs):
    gate = jnp.dot(rows[0][...], wv[0], preferred_element_type=F32)
    gate = 1.0 / (1.0 + jnp.exp(-gate))
    proj = jnp.dot(rows[1][...].astype(BF16), wv[1], preferred_element_type=F32)
    outs[0][...] = gate * proj


def _lambda(lq1_ref, lk1_ref, lq2_ref, lk2_ref, lam_init):
    s1 = jnp.sum(lq1_ref[...] * lk1_ref[...], axis=-1, keepdims=True)
    s2 = jnp.sum(lq2_ref[...] * lk2_ref[...], axis=-1, keepdims=True)
    return jnp.exp(s1) - jnp.exp(s2) + lam_init


def _subln(o, g, lam_init):
    y = o * lax.rsqrt(jnp.mean(o * o, axis=-1, keepdims=True) + SUBLN_EPS)
    return y * g * (1.0 - lam_init)


def _lane_tile(x, width):
    return jnp.concatenate([x] * (width // LANES), axis=1)


def _prompt_attn_kernel(it_ref, jt_ref, q_ref, k_ref, v_ref, lq1_ref, lk1_ref, lq2_ref, lk2_ref, g_ref,
                        o_ref, s_sc, p_sc, a_sc, m_sc, l_sc, acc_sc, *, lam_init, dk, strip):
    s = pl.program_id(1)
    i = it_ref[s]
    j = jt_ref[s]
    t = q_ref.shape[0]
    dv = v_ref.shape[1]

    @pl.when(j == 0)
    def _():
        m_sc[...] = jnp.full(m_sc.shape, NEG_INF, F32)
        l_sc[...] = jnp.zeros(l_sc.shape, F32)
        acc_sc[...] = jnp.zeros(acc_sc.shape, F32)

    def step(masked):
        for mp in range(2):
            s_sc[mp] = _dot_nt(q_ref[:, mp * dk:(mp + 1) * dk], k_ref[:, mp * dk:(mp + 1) * dk])

        v = v_ref[...]
        for mp in range(2):
            for r in range(t // strip):
                rows = slice(r * strip, (r + 1) * strip)
                sc = s_sc[mp, rows, :]
                if masked:
                    sc = jnp.where(_iota(sc.shape, 1) <= _iota(sc.shape, 0) + r * strip, sc, NEG_INF)
                m_prev = m_sc[mp, rows, :]
                m_new = jnp.maximum(m_prev, jnp.max(sc, axis=-1, keepdims=True))
                alpha = jnp.exp2(m_prev - m_new)
                p = jnp.exp2(sc - _lane_tile(m_new, t))
                l_sc[mp, rows, :] = alpha * l_sc[mp, rows, :] + jnp.sum(p, axis=-1, keepdims=True)
                a_sc[mp, rows, :] = alpha
                m_sc[mp, rows, :] = m_new
                p_sc[mp, rows, :] = p.astype(BF16)
            acc_sc[mp] = (_lane_tile(a_sc[mp], dv) * acc_sc[mp]
                          + jnp.dot(p_sc[mp], v, preferred_element_type=F32))

    @pl.when(j < i)
    def _():
        step(False)

    @pl.when(j == i)
    def _():
        step(True)
        lam = _lambda(lq1_ref, lk1_ref, lq2_ref, lk2_ref, lam_init)
        o = acc_sc[0] / _lane_tile(l_sc[0], dv) - lam * (acc_sc[1] / _lane_tile(l_sc[1], dv))
        o_ref[...] = _subln(o, g_ref[...], lam_init).astype(o_ref.dtype)


def _prompt_attention(q, k, v, lam_vecs, g, lam_init, heads, dk, dv):
    s_len = q.shape[0]
    t = _tile(s_len, ATTN_TILE)
    nb = s_len // t
    pairs = [(i, j) for i in range(nb) for j in range(i + 1)]
    it = jnp.asarray([p[0] for p in pairs], jnp.int32)
    jt = jnp.asarray([p[1] for p in pairs], jnp.int32)
    vec = pl.BlockSpec((1, dk), lambda h, s, it, jt: (0, 0))
    grid_spec = pltpu.PrefetchScalarGridSpec(
        num_scalar_prefetch=2,
        grid=(heads, len(pairs)),
        in_specs=[pl.BlockSpec((t, 2 * dk), lambda h, s, it, jt: (it[s], h)),
                  pl.BlockSpec((t, 2 * dk), lambda h, s, it, jt: (jt[s], h)),
                  pl.BlockSpec((t, dv), lambda h, s, it, jt: (jt[s], h)),
                  vec, vec, vec, vec,
                  pl.BlockSpec((1, dv), lambda h, s, it, jt: (0, 0))],
        out_specs=pl.BlockSpec((t, dv), lambda h, s, it, jt: (it[s], h)),
        scratch_shapes=[pltpu.VMEM((2, t, t), F32), pltpu.VMEM((2, t, t), BF16), pltpu.VMEM((2, t, LANES), F32),
                        pltpu.VMEM((2, t, LANES), F32), pltpu.VMEM((2, t, LANES), F32),
                        pltpu.VMEM((2, t, dv), F32)],
    )
    return pl.pallas_call(
        functools.partial(_prompt_attn_kernel, lam_init=lam_init, dk=dk, strip=min(ATTN_STRIP, t)),
        grid_spec=grid_spec,
        out_shape=jax.ShapeDtypeStruct((s_len, heads * dv), BF16),
        compiler_params=_cparams(("parallel", "arbitrary")),
        name="prompt_attention",
    )(it, jt, q, k, v, *lam_vecs, g.reshape(1, dv))


def _sample_attn_kernel(pt_ref, q_ref, kn_ref, vn_ref, *rest, lam_init, heads, dk, dv, pages):
    k_refs = rest[:pages]
    v_refs = rest[pages:2 * pages]
    (lq1_ref, lk1_ref, lq2_ref, lk2_ref, g_ref, o_ref, q_sc, bias_sc, m_sc, l_sc, acc_sc) = rest[2 * pages:]
    step_id = pl.program_id(1)
    half = SUBLANES // 2
    rp = heads * half
    page_lanes = bias_sc.shape[1]

    def own_head(shape):
        return _iota(shape, 1) % heads == (_iota(shape, 0) % rp) // half

    @pl.when(step_id == 0)
    def _():
        q = q_ref[0:SUBLANES, :]
        for par in range(2):
            for j in range(heads // 2):
                lo = q[:, (4 * j + par) * dk:(4 * j + par + 1) * dk]
                hi = q[:, (4 * j + 2 + par) * dk:(4 * j + 3 + par) * dk]
                q_sc[par * rp + j * SUBLANES:par * rp + (j + 1) * SUBLANES, :] = lo + pltpu.roll(hi, half, 0)
        bias_sc[...] = jnp.where(own_head(bias_sc.shape), 0.0, NEG_INF)
        m_sc[...] = jnp.full(m_sc.shape, NEG_INF, F32)
        l_sc[...] = jnp.zeros(l_sc.shape, F32)
        acc_sc[...] = jnp.zeros(acc_sc.shape, F32)

    def scores(k_ref, lanes):
        return jnp.concatenate(
            [_dot_nt(q_sc[par * rp:(par + 1) * rp, :], k_ref[pl.ds(par, lanes, stride=2), :]) for par in range(2)],
            axis=0)

    def update(sc, v_list):
        m_prev = m_sc[...]
        m_new = jnp.maximum(m_prev, jnp.max(sc, axis=-1, keepdims=True))
        alpha = jnp.exp2(m_prev - m_new)
        p = jnp.exp2(sc - m_new)
        l_sc[...] = alpha * l_sc[...] + jnp.sum(p, axis=-1, keepdims=True)
        pv, off = None, 0
        for vr in v_list:
            n = vr.shape[0]
            part = _dot(p[:, off:off + n], vr[...])
            pv = part if pv is None else pv + part
            off += n
        acc_sc[...] = alpha * acc_sc[...] + pv
        m_sc[...] = m_new

    bias = bias_sc[...]
    update(jnp.concatenate([scores(kr, page_lanes) + bias for kr in k_refs], axis=1), v_refs)

    @pl.when(step_id == pl.num_programs(1) - 1)
    def _():
        sc = scores(kn_ref, vn_ref.shape[0])
        ok = own_head(sc.shape) & (_iota(sc.shape, 1) // heads <= _iota(sc.shape, 0) % half)
        update(jnp.where(ok, sc, NEG_INF), [vn_ref])
        lam = _lambda(lq1_ref, lk1_ref, lq2_ref, lk2_ref, lam_init)
        accn = acc_sc[...] / l_sc[...]
        y = _subln(accn[:rp] - lam * accn[rp:], g_ref[...], lam_init)
        for j in range(heads // 2):
            y8 = y[j * SUBLANES:(j + 1) * SUBLANES]
            o_ref[:, (2 * j) * dv:(2 * j + 1) * dv] = y8
            o_ref[:, (2 * j + 1) * dv:(2 * j + 2) * dv] = pltpu.roll(y8, half, 0)


def _sample_attention(q, k_new, v_new, cache_k, cache_v, pool_offset, page_table, lam_vecs, g, lam_init, n_new):
    b, pad_t, width = q.shape
    n_pages = page_table.shape[1]
    dk, dv = cache_k.shape[2], cache_v.shape[2]
    maps = width // dk
    heads = maps // 2
    page = cache_k.shape[1] // maps
    half = SUBLANES // 2
    assert n_new <= half and pad_t >= SUBLANES and heads % 2 == 0
    pages = math.gcd(n_pages, PAGES_PER_STEP)
    rows = 2 * heads * half
    vec = pl.BlockSpec((1, dk), lambda bi, s, pt: (0, 0))
    per_row = lambda a: pl.BlockSpec((None,) + a.shape[1:], lambda bi, s, pt: (bi, 0, 0))

    def page_spec(pg, a):
        return pl.BlockSpec((None,) + a.shape[1:], lambda bi, s, pt: (pool_offset + pt[bi, s * pages + pg], 0, 0))

    grid_spec = pltpu.PrefetchScalarGridSpec(
        num_scalar_prefetch=1,
        grid=(b, n_pages // pages),
        in_specs=([per_row(q), per_row(k_new), per_row(v_new)] + [page_spec(pg, cache_k) for pg in range(pages)]
                  + [page_spec(pg, cache_v) for pg in range(pages)]
                  + [vec, vec, vec, vec, pl.BlockSpec((1, dv), lambda bi, s, pt: (0, 0))]),
        out_specs=pl.BlockSpec((None, SUBLANES, heads * dv), lambda bi, s, pt: (bi, 0, 0)),
        scratch_shapes=[pltpu.VMEM((rows, dk), F32), pltpu.VMEM((rows, page * heads), F32),
                        pltpu.VMEM((rows, 1), F32), pltpu.VMEM((rows, 1), F32), pltpu.VMEM((rows, dv), F32)],
    )
    return pl.pallas_call(
        functools.partial(_sample_attn_kernel, lam_init=lam_init, heads=heads, dk=dk, dv=dv, pages=pages),
        grid_spec=grid_spec,
        out_shape=jax.ShapeDtypeStruct((b, SUBLANES, heads * dv), F32),
        compiler_params=_cparams(("parallel", "arbitrary")),
        name="sample_attention",
    )(page_table, q, k_new, v_new, *([cache_k] * pages), *([cache_v] * pages), *lam_vecs, g.reshape(1, dv))


def _segsums(xs, ones_bd):
    parts = []
    for x in xs:
        hi = x.astype(BF16)
        parts += [hi, (x - hi.astype(F32)).astype(BF16)]
    s = jnp.dot(jnp.concatenate(parts, axis=0), ones_bd, preferred_element_type=F32)
    m = xs[0].shape[0]
    return [s[2 * i * m:(2 * i + 1) * m] + s[(2 * i + 1) * m:(2 * i + 2) * m] for i in range(len(xs))]


def _segsum(x, ones_bd):
    return _segsums([x], ones_bd)[0]


def _block_ones(w):
    return jnp.where(_iota((w, w), 0) // RW_N == _iota((w, w), 1) // RW_N, 1.0, 0.0).astype(BF16)


def _rw_prep_kernel(z_ref, zprev_ref, mu_ref, w0_ref, w2_ref, a0_ref, a2_ref, g2_ref, kk_ref, ka_ref,
                    r_out, lw_out, k_out, v_out, kk_out, b_out, g_out, carry_sc, *, mode, period, rw, dd, da, gp):
    z = z_ref[...]
    tm = z.shape[0]
    rolled = pltpu.roll(z, 1, 0)
    row = _iota(z.shape, 0)
    if mode == "carry":
        @pl.when(pl.program_id(0) == 0)
        def _():
            carry_sc[...] = zprev_ref[...]
        zs = jnp.where(row == 0, carry_sc[...], rolled)
        carry_sc[...] = z[tm - 1:tm, :]
    else:
        zs = jnp.where(row % period == 0, zprev_ref[...], rolled)
    zz = z + (zs - z) * mu_ref[...]
    r = zz[:, :rw]
    kr = zz[:, rw:2 * rw]
    v = zz[:, 2 * rw:3 * rw]
    zw = zz[:, 3 * rw:3 * rw + dd]
    za = zz[:, 3 * rw + dd:3 * rw + dd + da]
    zg = zz[:, 3 * rw + dd + da:3 * rw + dd + da + gp]
    u = w0_ref[...] + _dot(jnp.tanh(zw), w2_ref[...])
    w_log = jnp.minimum(u, 0.0) - jnp.log(1.0 + jnp.exp(-jnp.abs(u))) - 0.5
    lw_out[...] = -jnp.exp(w_log)
    a = 1.0 / (1.0 + jnp.exp(-(a0_ref[...] + _dot(za, a2_ref[...]))))
    g_out[...] = _dot(1.0 / (1.0 + jnp.exp(-zg)), g2_ref[...])
    ones_bd = _block_ones(LANES)
    kk = kr * kk_ref[...]
    for c in range(rw // LANES):
        sl = slice(c * LANES, (c + 1) * LANES)
        kc = kk[:, sl]
        kc = kc * lax.rsqrt(jnp.maximum(_segsum(kc * kc, ones_bd), 1e-24))
        kk_out[:, sl] = kc
        b_out[:, sl] = kc * a[:, sl]
    r_out[...] = r
    v_out[...] = v
    k_out[...] = kr * (1.0 + (a - 1.0) * ka_ref[...])


def _rw_prep(z, zprev, prm, mode, period):
    m, zp = z.shape
    rw, dd, da, gp = prm["rw"], prm["dd"], prm["da"], prm["gp"]
    tm = _tile(m, 128)
    row = pl.BlockSpec((tm, zp), lambda i: (i, 0))
    full = lambda a: pl.BlockSpec(a.shape, lambda i: (0,) * a.ndim)
    zprev_spec = full(zprev) if mode == "carry" else row
    params = [prm["mu"], prm["w0"], prm["w2"], prm["a0"], prm["a2"], prm["g2"], prm["k_k"], prm["k_a"]]
    out_spec = pl.BlockSpec((tm, rw), lambda i: (i, 0))
    return pl.pallas_call(
        functools.partial(_rw_prep_kernel, mode=mode, period=period, rw=rw, dd=dd, da=da, gp=gp),
        grid=(m // tm,),
        in_specs=[row, zprev_spec] + [full(p) for p in params],
        out_specs=[out_spec] * 7,
        out_shape=[jax.ShapeDtypeStruct((m, rw), F32)] * 7,
        scratch_shapes=[pltpu.VMEM((1, zp), F32)],
        compiler_params=_cparams(("arbitrary",)),
        name="rwkv_prep",
    )(z, zprev, *params)


def _bd(x, gw):
    head = (_iota(x.shape, 1) % gw) // RW_N
    return jnp.concatenate([jnp.where(head == h, x, 0.0) for h in range(gw // RW_N)], axis=0)


def _rw_scan_kernel(*refs, has_s0, t_valid, gw, n_groups):
    if has_s0:
        (r_ref, lw_ref, k_ref, v_ref, kk_ref, b_ref, g_ref, rk_ref, lng_ref, lnb_ref, s0_ref,
         y_out, s_out, sb_sc) = refs
    else:
        (r_ref, lw_ref, k_ref, v_ref, kk_ref, b_ref, g_ref, rk_ref, lng_ref, lnb_ref,
         y_out, s_out, sb_sc) = refs
    c = RW_CHUNK
    n = RW_N
    tc = r_ref.shape[0]
    ci = pl.program_id(1)
    last = pl.num_programs(1) - 1

    tri_inc = jnp.where(_iota((c, c), 1) <= _iota((c, c), 0), 1.0, 0.0).astype(BF16)
    s_idx = _iota((c, gw), 1) % n
    t_idx = _iota((c, gw), 0)
    strict = s_idx < t_idx
    incl = s_idx <= t_idx
    eye_w = jnp.where(s_idx == t_idx, 1.0, 0.0)
    gi0, gi1 = _iota((gw, gw), 0), _iota((gw, gw), 1)
    blk = gi0 // n == gi1 // n
    eye_g = jnp.where(gi0 == gi1, 1.0, 0.0)
    ones_bd = jnp.where(blk, 1.0, 0.0).astype(BF16)
    sel = jnp.where(_iota((gw, n), 0) % n == _iota((gw, n), 1), 1.0, 0.0).astype(BF16)
    sel_t = jnp.where(_iota((n, gw), 1) % n == _iota((n, gw), 0), 1.0, 0.0).astype(BF16)
    valid = _iota((tc, gw), 0) < t_valid

    def load(ref, sl, fill=0.0):
        x = ref[:, sl]
        if t_valid < tc:
            x = jnp.where(valid, x, fill)
        if tc < c:
            x = jnp.concatenate([x, jnp.full((c - tc, gw), fill, F32)], axis=0)
        return x

    @pl.when(ci == 0)
    def _():
        for gi in range(n_groups):
            if has_s0:
                sb_sc[gi] = jnp.where(blk, _dot_sel_r(s0_ref[gi * gw:(gi + 1) * gw, :], sel_t), 0.0)
            else:
                sb_sc[gi] = jnp.zeros((gw, gw), F32)

    def group_chain(gi):
        sl = slice(gi * gw, (gi + 1) * gw)
        r, lw, k, v, kk, b = (load(ref, sl) for ref in (r_ref, lw_ref, k_ref, v_ref, kk_ref, b_ref))
        cl = _dot_sel(tri_inc, lw)
        yield
        cend = cl[c - 1:c, :]
        kt = kk * jnp.exp(cl - lw)
        rt = r * jnp.exp(cl)
        g_inv = jnp.exp(-cl)
        g_rat = jnp.exp(cend - cl)
        ktil, btil = k * g_inv, b * g_inv
        khat, bhat = k * g_rat, b * g_rat
        la = _dot_nt(jnp.concatenate([kt, rt], axis=0),
                     jnp.concatenate([_bd(btil, gw), _bd(ktil, gw)], axis=0))
        yield
        lb = jnp.where(strict, la[:c, :gw], 0.0)
        lk = jnp.where(strict, la[:c, gw:], 0.0)
        ab = jnp.where(incl, la[c:, :gw], 0.0)
        ak = jnp.where(incl, la[c:, gw:], 0.0)
        lav = _dot(jnp.concatenate([lk, ak], axis=0), _bd(v, gw))
        lkv, y0 = lav[:c], lav[c:]
        qt = _dot_tn(v, khat)
        x = eye_w - lb
        rounds = max(0, math.ceil(math.log2(t_valid)) - 1)
        if rounds:
            p = _dot(lb, _bd(lb, gw))
            yield
        for rnd in range(rounds):
            if rnd == rounds - 1:
                x = x + _dot(x, _bd(p, gw))
            else:
                px = _dot(jnp.concatenate([p, x], axis=0), _bd(p, gw))
                p, x = px[:c], x + px[c:]
            yield
        tw = _dot(x, _bd(jnp.concatenate([kt, lkv], axis=1), gw))
        yield
        abw = _dot(ab, _bd(tw, gw))
        tn = _dot_tn(tw, bhat)
        yield
        rr = rt - abw[:, :gw]
        y0 = y0 - abw[:, gw:]
        pt = eye_g * jnp.exp(cend) - jnp.where(blk, tn[:gw], 0.0)
        qt = jnp.where(blk, qt - tn[gw:], 0.0)
        s_old = sb_sc[gi]
        y = _dot_nt(rr, s_old) + y0
        sb_sc[gi] = _dot(s_old, pt) + qt
        yield
        y_sum, rk_sum = _segsums([y, r * k * rk_ref[:, sl]], ones_bd)
        bonus = rk_sum * v
        yield
        d = y - y_sum * (1.0 / n)
        var = _segsum(d * d, ones_bd) * (1.0 / n)
        yield
        yn = d * lax.rsqrt(var + RW_LN_EPS) * lng_ref[:, sl] + lnb_ref[:, sl]
        out = (yn + bonus)[:tc] * g_ref[:, sl]
        y_out[:, sl] = out.astype(y_out.dtype)

    chains = [group_chain(gi) for gi in range(n_groups)]
    while chains:
        chains = [ch for ch in chains if next(ch, "done") != "done"]

    @pl.when(ci == last)
    def _():
        for gi in range(n_groups):
            s_out[gi * gw:(gi + 1) * gw, :] = _dot_sel_r(sb_sc[gi], sel)


def _rw_scan(r, lw, k, v, kk, b, g, prm, s0, n_seq, tc, t_valid):
    m, rw = r.shape
    n_chunks = m // (n_seq * tc)
    gw = math.gcd(rw, RW_GROUP)
    n_groups = rw // gw
    row = pl.BlockSpec((tc, rw), lambda s, ci: (s * n_chunks + ci, 0))
    vec = pl.BlockSpec((1, rw), lambda s, ci: (0, 0))
    state = pl.BlockSpec((None, rw, RW_N), lambda s, ci: (s, 0, 0))
    has_s0 = s0 is not None
    ins = [r, lw, k, v, kk, b, g, prm["r_k"], prm["ln_g"], prm["ln_b"]] + ([s0] if has_s0 else [])
    return pl.pallas_call(
        functools.partial(_rw_scan_kernel, has_s0=has_s0, t_valid=t_valid, gw=gw, n_groups=n_groups),
        grid=(n_seq, n_chunks),
        in_specs=[row] * 7 + [vec] * 3 + ([state] if has_s0 else []),
        out_specs=[row, state],
        out_shape=[jax.ShapeDtypeStruct((m, rw), BF16), jax.ShapeDtypeStruct((n_seq, rw, RW_N), F32)],
        scratch_shapes=[pltpu.VMEM((n_groups, gw, gw), F32)],
        compiler_params=_cparams(("parallel", "arbitrary")),
        name="rwkv_scan",
    )(*ins)


def _rope_tables(pos, dk):
    half = dk // 2
    inv = ROPE_THETA ** (-jnp.arange(half, dtype=F32) / half)
    ang = pos.astype(F32)[:, None] * inv[None, :]
    cos, sin = jnp.cos(ang), jnp.sin(ang)
    return jnp.concatenate([cos, cos], axis=-1), jnp.concatenate([-sin, sin], axis=-1)


def _prep_weights(i, dims, w_in, rw_mu, rw_w0, rw_w2, rw_a0, rw_a2, rw_g2, rw_k_k, rw_k_a, rw_r_k, rw_ln_g,
                  rw_ln_b):
    qw, da_w, rw, dd, da, dg, gp, zp = (dims[n] for n in ("qw", "da_w", "rw", "dd", "da", "dg", "gp", "zp"))
    rw_cols = 3 * rw + dd + da + dg
    w = jnp.pad(w_in[i], ((0, 0), (0, zp - rw_cols))).astype(BF16)[None]
    prm = {
        "rw": rw, "dd": dd, "da": da, "gp": gp,
        "mu": jnp.pad(rw_mu[i], (0, zp - rw_cols)).reshape(1, zp),
        "w0": rw_w0[i].reshape(1, rw), "w2": rw_w2[i].astype(BF16),
        "a0": rw_a0[i].reshape(1, rw), "a2": rw_a2[i].astype(BF16),
        "g2": jnp.pad(rw_g2[i], ((0, gp - dg), (0, 0))).astype(BF16),
        "k_k": rw_k_k[i].reshape(1, rw), "k_a": rw_k_a[i].reshape(1, rw),
        "r_k": rw_r_k[i].reshape(1, rw), "ln_g": rw_ln_g[i].reshape(1, rw), "ln_b": rw_ln_b[i].reshape(1, rw),
    }
    return w, prm


def _tail(i, xs, os_, rwos, pes, w_out_b, w_up, w_down, w_gate_b, w_ple_proj, gains):
    g_post_mix, g_pre_ffn, g_post_ffn, g_pre_ple, g_post_ple = gains
    d = xs[0].shape[1]
    d_ff = w_up.shape[2]
    half = os_[0].shape[1]
    assert rwos[0].shape[1] == half and w_out_b.shape[1] == 2 * half
    tn = _tile(d, MM_TILE)
    *mix, w_up_b = _dense(_mm2_fn, [(os_[0], os_[1], False), (rwos[0], rwos[1], False)],
                          [(w_out_b, 0, half, 0, 0), (w_out_b, 0, half, 1, 0)], [(F32, None, F32)], n=d, tn=tn,
                          tm=MM_TILE // 2, cast_src=(w_up, i), name="out_proj")
    x, h = zip(*(_norm_residual(f[0], xx, g_post_mix, g_pre_ffn) for f, xx in zip(mix, xs)))
    tk = _tile(d_ff, FFN_TK)
    (u,), _, w_down_b = _dense(_ffn_up_fn, [(h[0], None, False)], [(w_up_b, 0, d, 0, 0)], [(BF16, None, None)],
                               n=d_ff, tn=_tile(d_ff, MM_TILE), cast_src=(w_down, i), name="ffn_up")
    (u_s,), _ = _dense(_ffn_up_fn, [(h[1], None, False)], [(w_up_b, 0, d, 0, 0)], [(BF16, None, None)],
                       n=d_ff, tn=_tile(d_ff, MM_TILE), name="ffn_up")
    f = [_dense(_mm_acc_fn, [(uu, None, True)], [(w_down_b, 0, tk, 0, 0)], [(F32, None, None)],
                n=d, tn=tn, k_steps=d_ff // tk, name="ffn_down")[0] for uu in (u, u_s)]
    x, h = zip(*(_norm_residual(ff[0], xx, g_post_ffn, g_pre_ple) for ff, xx in zip(f, x)))
    e = _dense(_ple_fn, [(h[0], h[1], False), (pes[0], pes[1], False)],
               [(w_gate_b, 0, d, 0, 0), (w_ple_proj.astype(BF16), i, pes[0].shape[1], 0, 0)], [(F32, None, F32)],
               n=d, tn=tn, tm=MM_TILE // 2, name="ple")
    return tuple(_norm_residual(ee[0], xx, g_post_ple, None)[0] for ee, xx in zip(e, x))


def kernel(x_prompt, x_sample, cache_k, cache_v, state_wkv, state_shift, page_table, p_prompt, p_sample, g_pre_mix, w_in, lam_q1, lam_k1, lam_q2, lam_k2, da_subln_g, rw_mu, rw_w0, rw_w2, rw_a0, rw_a2, rw_g2, rw_k_k, rw_k_a, rw_r_k, rw_ln_g, rw_ln_b, w_out, g_post_mix, g_pre_ffn, w_up, w_down, g_post_ffn, g_pre_ple, w_ple_gate, w_ple_proj, g_post_ple):
    batch, seq, d = x_prompt.shape
    db, dseq, _ = x_sample.shape
    depth, n_pool, page, maps, dk = cache_k.shape
    heads, dv = cache_v.shape[3], cache_v.shape[4]
    rwh, rwn = rw_r_k.shape[1], rw_r_k.shape[2]
    assert maps == 2 * heads and dk == LANES and rwn == RW_N and seq % RW_CHUNK == 0
    qw, da_w, rw = maps * dk, heads * dv, rwh * rwn
    dd, da, dg = rw_w2.shape[1], rw_a2.shape[1], rw_g2.shape[1]
    assert dd % LANES == 0 and da % LANES == 0 and rw % LANES == 0
    gp = _round_up(dg, LANES)
    rw_cols = 3 * rw + dd + da + dg
    zp = _round_up(3 * rw + dd + da + gp, min(MM_TILE, _round_up(rw_cols, LANES)))
    dims = dict(qw=qw, da_w=da_w, rw=rw, dd=dd, da=da, dg=dg, gp=gp, zp=zp)
    n_pages = page_table.shape[1]
    past_len = n_pages * page
    pad_t = _round_up(dseq, SUBLANES)
    pad_a = _round_up(dseq, 2 * SUBLANES)
    scale = dk ** -0.5

    cos_p, sin_p = _rope_tables(jnp.tile(jnp.arange(seq, dtype=jnp.int32), batch), dk)
    cos_s, sin_s = _rope_tables(jnp.tile(past_len + jnp.arange(dseq, dtype=jnp.int32), db), dk)

    xp = x_prompt.reshape(batch * seq, d)
    xs = x_sample.reshape(db * dseq, d)
    outs = [[] for _ in range(8)]
    for i in range(depth):
        lam_init = 0.8 - 0.6 * math.exp(-0.3 * i)
        w, prm = _prep_weights(i, dims, w_in, rw_mu, rw_w0, rw_w2, rw_a0, rw_a2, rw_g2, rw_k_k, rw_k_a, rw_r_k,
                               rw_ln_g, rw_ln_b)
        lam_vecs = [v[i].reshape(1, dk) for v in (lam_q1, lam_k1, lam_q2, lam_k2)]
        gains = (g_post_mix[i], g_pre_ffn[i], g_post_ffn[i], g_pre_ple[i], g_post_ple[i])

        h = _rmsnorm(xp, g_pre_mix[i], NORM_EPS)
        h_s = _rmsnorm(xs, g_pre_mix[i], NORM_EPS)
        roped = [(h, h_s, False), (cos_p, cos_s, False), (sin_p, sin_s, False)]
        tn_q, tn_z = _tile(qw, MM_TILE), _tile(zp, MM_TILE)
        assert qw % tn_q == 0 and (2 * qw) % da_w == 0 and (2 * qw + da_w) % tn_z == 0
        (q,), (q_s,), w_gate_b = _dense(
            functools.partial(_proj_fn, rope=True, scale=scale * math.log2(math.e), hw=dk),
            roped, [(w, 0, d, 0, 0)], [(BF16, None, F32)], n=qw, tn=tn_q, tm=MM_TILE // 2,
            cast_src=(w_ple_gate, i), name="proj_q")
        (k32, k16), (k32_s,), w_out_b = _dense(
            functools.partial(_proj_fn, rope=True, scale=1.0, hw=dk),
            roped, [(w, 0, d, 0, qw // tn_q)], [(F32, dk, F32), (BF16, None, None)], n=qw, tn=tn_q,
            tm=MM_TILE // 2, cast_src=(w_out, i), name="proj_k")
        v_fn = functools.partial(_proj_fn, rope=False, scale=1.0, hw=dv)
        v_w = [(w, 0, d, 0, 2 * qw // da_w)]
        (v32, v16), _ = _dense(v_fn, [(h, None, False)], v_w, [(F32, dv, None), (BF16, None, None)], n=da_w,
                               tn=da_w, tm=MM_TILE // 2, name="proj_v")
        (v32_s,), _ = _dense(v_fn, [(h_s, None, False)], v_w, [(F32, None, None)], n=da_w, tn=da_w, name="proj_v")
        (z,), (z_s,) = _dense(functools.partial(_proj_fn, rope=False, scale=1.0, hw=tn_z), [(h, h_s, False)],
                              [(w, 0, d, 0, (2 * qw + da_w) // tn_z)], [(F32, None, F32)], n=zp, tn=tn_z,
                              name="proj_z")

        o_rows, rw_rows, wkv_rows = [], [], []
        for bi in range(batch):
            rs = slice(bi * seq, (bi + 1) * seq)
            o_rows.append(_prompt_attention(q[rs], k16[rs], v16[rs], lam_vecs, da_subln_g[i], lam_init,
                                            heads, dk, dv))
            mixer_in = _rw_prep(z[rs], jnp.zeros((1, zp), F32), prm, "carry", 0)
            rwo, s_fin = _rw_scan(*mixer_in, prm, None, 1, RW_CHUNK, RW_CHUNK)
            rw_rows.append(rwo)
            wkv_rows.append(s_fin.reshape(rwh, rwn, rwn))
        o = o_rows[0] if batch == 1 else jnp.concatenate(o_rows, axis=0)
        rwo = rw_rows[0] if batch == 1 else jnp.concatenate(rw_rows, axis=0)
        outs[0].append(k32.reshape(batch, seq, maps, dk))
        outs[1].append(v32.reshape(batch, seq, heads, dv))
        outs[4].append(jnp.stack(wkv_rows, 0))
        outs[6].append(z.reshape(batch, seq, zp)[:, -1, :rw_cols])

        pad3 = lambda a, t: jnp.pad(a.reshape(db, dseq, -1), ((0, 0), (0, t - dseq), (0, 0)))
        o_s = _sample_attention(pad3(q_s, pad_a), pad3(k32_s, pad_a).reshape(db, pad_a * maps, dk),
                                pad3(v32_s, pad_a).reshape(db, pad_a * heads, dv),
                                cache_k.reshape(depth * n_pool, page * maps, dk),
                                cache_v.reshape(depth * n_pool, page * heads, dv), i * n_pool, page_table,
                                lam_vecs, da_subln_g[i], lam_init, dseq)
        o_s = o_s[:, :dseq].reshape(db * dseq, da_w).astype(BF16)
        zprev = jnp.zeros((db, pad_t, zp), F32).at[:, 0, :rw_cols].set(state_shift[i])
        mixer_in = _rw_prep(pad3(z_s, pad_t).reshape(db * pad_t, zp), zprev.reshape(db * pad_t, zp), prm, "rows",
                            pad_t)
        rwo_s, s_fin = _rw_scan(*mixer_in, prm, state_wkv[i].reshape(db, rw, rwn), db, pad_t, dseq)
        rwo_s = rwo_s.reshape(db, pad_t, rw)[:, :dseq].reshape(db * dseq, rw)
        outs[2].append(k32_s.reshape(db, dseq, maps, dk))
        outs[3].append(v32_s.reshape(db, dseq, heads, dv))
        outs[5].append(s_fin.reshape(db, rwh, rwn, rwn))
        outs[7].append(z_s.reshape(db, dseq, zp)[:, -1, :rw_cols])

        xp, xs = _tail(i, (xp, xs), (o, o_s), (rwo, rwo_s),
                       (p_prompt[i].reshape(batch * seq, -1), p_sample[i].reshape(db * dseq, -1)),
                       w_out_b, w_up, w_down, w_gate_b, w_ple_proj, gains)

    st =[jnp.stack(o, 0) for o in outs]
    return (xp.reshape(batch, seq, d), xs.reshape(db, dseq, d), st[0], st[1], st[2], st[3], st[4], st[5],
            st[6], st[7])
```

```python
import functools
import math

import jax
import jax.numpy as jnp
from jax import lax
from jax.experimental import pallas as pl
from jax.experimental.pallas import tpu as pltpu

F32 = jnp.float32
BF16 = jnp.bfloat16

LANES = 128
SUBLANES = 8
VMEM_LIMIT_BYTES = 56 * 1024 * 1024

ROPE_THETA = 10000.0
NORM_EPS = 1e-6
SUBLN_EPS = 1e-5
RW_LN_EPS = 64e-5
NEG_INF = -1e30

RW_N = 64
RW_CHUNK = 64
RW_GROUP = 256
MM_TILE = 1024
FFN_TK = 4096
ATTN_TILE = 1024
ATTN_STRIP = 32
PAGES_PER_STEP = 8


def _cparams(sem):
    return pltpu.CompilerParams(dimension_semantics=sem, vmem_limit_bytes=VMEM_LIMIT_BYTES)


def _round_up(x, m):
    return (x + m - 1) // m * m


def _tile(n, pref):
    t = min(n, pref)
    assert n % t == 0, (n, t)
    return t


def _dot(a, b):
    return jnp.dot(a.astype(BF16), b.astype(BF16), preferred_element_type=F32)


def _dot_nt(a, b):
    return lax.dot_general(a.astype(BF16), b.astype(BF16), (((1,), (1,)), ((), ())),
                           preferred_element_type=F32)


def _dot_tn(a, b):
    return lax.dot_general(a.astype(BF16), b.astype(BF16), (((0,), (0,)), ((), ())),
                           preferred_element_type=F32)


def _split3(x):
    hi = x.astype(BF16)
    r1 = x - hi.astype(F32)
    mid = r1.astype(BF16)
    lo = (r1 - mid.astype(F32)).astype(BF16)
    return hi, mid, lo


def _dot_sel(sel, x):
    hi, mid, lo = _split3(x)
    n = x.shape[1]
    y = jnp.dot(sel, jnp.concatenate([hi, mid, lo], axis=1), preferred_element_type=F32)
    return y[:, :n] + y[:, n:2 * n] + y[:, 2 * n:]


def _dot_sel_r(x, sel):
    hi, mid, lo = _split3(x)
    m = x.shape[0]
    y = jnp.dot(jnp.concatenate([hi, mid, lo], axis=0), sel, preferred_element_type=F32)
    return y[:m] + y[m:2 * m] + y[2 * m:]


def _iota(shape, dim):
    return lax.broadcasted_iota(jnp.int32, shape, dim)


def _rmsnorm_kernel(x_ref, g_ref, o_ref, *, eps):
    x = x_ref[...]
    y = x * lax.rsqrt(jnp.mean(x * x, axis=-1, keepdims=True) + eps)
    o_ref[...] = (y * g_ref[...]).astype(o_ref.dtype)


def _rmsnorm(x, g, eps):
    m, d = x.shape
    tm = _tile(m, 256)
    return pl.pallas_call(
        functools.partial(_rmsnorm_kernel, eps=eps),
        grid=(m // tm,),
        in_specs=[pl.BlockSpec((tm, d), lambda i: (i, 0)), pl.BlockSpec((1, d), lambda i: (0, 0))],
        out_specs=pl.BlockSpec((tm, d), lambda i: (i, 0)),
        out_shape=jax.ShapeDtypeStruct((m, d), BF16),
        compiler_params=_cparams(("parallel",)),
        name="rmsnorm",
    )(x, g.reshape(1, d))


def _norm_residual_kernel(f_ref, x_ref, gpost_ref, *rest, has_next):
    f = f_ref[...]
    y = f * lax.rsqrt(jnp.mean(f * f, axis=-1, keepdims=True) + NORM_EPS) * gpost_ref[...]
    xn = x_ref[...] + y
    if has_next:
        gnext_ref, xo_ref, ho_ref = rest
        xo_ref[...] = xn
        h = xn * lax.rsqrt(jnp.mean(xn * xn, axis=-1, keepdims=True) + NORM_EPS) * gnext_ref[...]
        ho_ref[...] = h.astype(ho_ref.dtype)
    else:
        (xo_ref,) = rest
        xo_ref[...] = xn


def _norm_residual(f, x, g_post, g_next):
    m, d = x.shape
    tm = _tile(m, 256)
    row = pl.BlockSpec((tm, d), lambda i: (i, 0))
    vec = pl.BlockSpec((1, d), lambda i: (0, 0))
    has_next = g_next is not None
    ins = [f, x, g_post.reshape(1, d)] + ([g_next.reshape(1, d)] if has_next else [])
    out = pl.pallas_call(
        functools.partial(_norm_residual_kernel, has_next=has_next),
        grid=(m // tm,),
        in_specs=[row, row, vec] + ([vec] if has_next else []),
        out_specs=[row, row] if has_next else [row],
        out_shape=([jax.ShapeDtypeStruct((m, d), F32), jax.ShapeDtypeStruct((m, d), BF16)]
                   if has_next else [jax.ShapeDtypeStruct((m, d), F32)]),
        compiler_params=_cparams(("parallel",)),
        name="norm_residual",
    )(*ins)
    return (out[0], out[1]) if has_next else (out[0], None)


def _dense_kernel(*refs, fn, n_rows, n_w, n_main_out, has_side, has_cast):
    refs = list(refs)
    if has_cast:
        cast_in, cast_out = refs.pop(n_rows * (2 if has_side else 1) + n_w), refs.pop()
        cast_out[...] = cast_in[...].astype(cast_out.dtype)
    main_rows = refs[:n_rows]
    side_rows = refs[n_rows:2 * n_rows] if has_side else []
    pos = n_rows * (2 if has_side else 1)
    w_refs = refs[pos:pos + n_w]
    main_outs = refs[pos + n_w:pos + n_w + n_main_out]
    side_outs = refs[pos + n_w + n_main_out:]
    wv = [w[...].astype(BF16) for w in w_refs]
    fn(main_rows, wv, main_outs)
    if has_side:
        @pl.when(pl.program_id(0) == 0)
        def _():
            fn(side_rows, wv, side_outs)


def _dense(fn, row_ins, w_ins, out_defs, *, n, tn, tm=MM_TILE, k_steps=1, cast_src=None, name):
    m = row_ins[0][0].shape[0]
    has_side = row_ins[0][1] is not None
    tm = _tile(m, tm)
    nj = n // tn
    w_mode = dict(pipeline_mode=pl.Buffered(1)) if nj == 1 and k_steps == 1 else {}
    in_specs, ins = [], []
    for which in ((0, 1) if has_side else (0,)):
        for entry in row_ins:
            a, k_tiled = entry[which], entry[2]
            rows = tm if which == 0 else a.shape[0]
            width = a.shape[1] // k_steps if k_tiled else a.shape[1]
            if which == 0:
                imap = (lambda i, j, kk: (i, kk)) if k_tiled else (lambda i, j, kk: (i, 0))
            else:
                imap = (lambda i, j, kk: (0, kk)) if k_tiled else (lambda i, j, kk: (0, 0))
            in_specs.append(pl.BlockSpec((rows, width), imap))
            ins.append(a)
    for w, layer, k_rows, rb, cb in w_ins:
        in_specs.append(pl.BlockSpec((None, k_rows, tn),
                                     lambda i, j, kk, layer=layer, rb=rb, cb=cb: (layer, rb + kk, cb + j), **w_mode))
        ins.append(w)
    out_specs, out_shape = [], []
    for dt, hw, _ in out_defs:
        if hw is None:
            out_specs.append(pl.BlockSpec((tm, tn), lambda i, j, kk: (i, j)))
            out_shape.append(jax.ShapeDtypeStruct((m, n), dt))
        else:
            out_specs.append(pl.BlockSpec((tm, tn // hw, hw), lambda i, j, kk: (i, j, 0)))
            out_shape.append(jax.ShapeDtypeStruct((m, n // hw, hw), dt))
    n_main_out = len(out_specs)
    if has_side:
        ms = row_ins[0][1].shape[0]
        for _, _, side_dt in out_defs:
            if side_dt is not None:
                out_specs.append(pl.BlockSpec((ms, tn), lambda i, j, kk: (0, jnp.where(i == 0, j, nj - 1))))
                out_shape.append(jax.ShapeDtypeStruct((ms, n), side_dt))
    if cast_src is not None:
        src, layer = cast_src
        steps = (m // tm) * nj
        assert k_steps == 1 and src.shape[1] % steps == 0
        slab = src.shape[1] // steps
        in_specs.append(pl.BlockSpec((None, slab, src.shape[2]),
                                     lambda i, j, kk, layer=layer: (layer, i * nj + j, 0)))
        ins.append(src)
        out_specs.append(pl.BlockSpec((slab, src.shape[2]), lambda i, j, kk: (i * nj + j, 0)))
        out_shape.append(jax.ShapeDtypeStruct(src.shape[1:], BF16))
    out = pl.pallas_call(
        functools.partial(_dense_kernel, fn=fn, n_rows=len(row_ins), n_w=len(w_ins), n_main_out=n_main_out,
                          has_side=has_side, has_cast=cast_src is not None),
        grid=(m // tm, nj, k_steps),
        in_specs=in_specs,
        out_specs=out_specs,
        out_shape=out_shape,
        compiler_params=_cparams(("arbitrary", "arbitrary", "arbitrary")),
        name=name,
    )(*ins)
    if cast_src is not None:
        return out[:n_main_out], out[n_main_out:-1], out[-1][None]
    return out[:n_main_out], out[n_main_out:]


def _proj_fn(rows, wv, outs, *, rope, scale, hw, n_valid=None):
    acc = jnp.dot(rows[0][...], wv[0], preferred_element_type=F32)
    if n_valid is not None:
        col = pl.program_id(1) * acc.shape[1] + _iota(acc.shape, 1)
        acc = jnp.where(col < n_valid, acc, 0.0)
    if rope:
        cos = rows[1][...]
        sin = rows[2][...]
    for g in range(acc.shape[1] // hw):
        x = acc[:, g * hw:(g + 1) * hw]
        if rope:
            x = x * cos + pltpu.roll(x, hw // 2, 1) * sin
        if scale != 1.0:
            x = x * scale
        for o in outs:
            if len(o.shape) == 3:
                o[:, g, :] = x.astype(o.dtype)
            else:
                o[:, g * hw:(g + 1) * hw] = x.astype(o.dtype)


def _mm2_fn(rows, wv, outs):
    outs[0][...] = (jnp.dot(rows[0][...], wv[0], preferred_element_type=F32)
                    + jnp.dot(rows[1][...], wv[1], preferred_element_type=F32))


def _ffn_up_fn(rows, wv, outs):
    u = jnp.maximum(jnp.dot(rows[0][...], wv[0], preferred_element_type=F32), 0.0)
    outs[0][...] = (u * u).astype(outs[0].dtype)


def _mm_acc_fn(rows, wv, outs):
    part = jnp.dot(rows[0][...], wv[0], preferred_element_type=F32)

    @pl.when(pl.program_id(2) == 0)
    def _():
        outs[0][...] = part

    @pl.when(pl.program_id(2) != 0)
    def _():
        outs[0][...] += part


def _ple_fn(rows, wv, outs):
    gate = jnp.dot(rows[0][...], wv[0], preferred_element_type=F32)
    gate = 1.0 / (1.0 + jnp.exp(-gate))
    proj = jnp.dot(rows[1][...].astype(BF16), wv[1], preferred_element_type=F32)
    outs[0][...] = gate * proj


def _lambda(lq1_ref, lk1_ref, lq2_ref, lk2_ref, lam_init):
    s1 = jnp.sum(lq1_ref[...] * lk1_ref[...], axis=-1, keepdims=True)
    s2 = jnp.sum(lq2_ref[...] * lk2_ref[...], axis=-1, keepdims=True)
    return jnp.exp(s1) - jnp.exp(s2) + lam_init


def _subln(o, g, lam_init):
    y = o * lax.rsqrt(jnp.mean(o * o, axis=-1, keepdims=True) + SUBLN_EPS)
    return y * g * (1.0 - lam_init)


def _lane_tile(x, width):
    return jnp.concatenate([x] * (width // LANES), axis=1)


def _prompt_attn_kernel(it_ref, jt_ref, q_ref, k_ref, v_ref, lq1_ref, lk1_ref, lq2_ref, lk2_ref, g_ref,
                        o_ref, s_sc, p_sc, a_sc, m_sc, l_sc, acc_sc, *, lam_init, dk, strip):
    s = pl.program_id(1)
    i = it_ref[s]
    j = jt_ref[s]
    t = q_ref.shape[0]
    dv = v_ref.shape[1]

    @pl.when(j == 0)
    def _():
        m_sc[...] = jnp.full(m_sc.shape, NEG_INF, F32)
        l_sc[...] = jnp.zeros(l_sc.shape, F32)
        acc_sc[...] = jnp.zeros(acc_sc.shape, F32)

    def step(masked):
        for mp in range(2):
            s_sc[mp] = _dot_nt(q_ref[:, mp * dk:(mp + 1) * dk], k_ref[:, mp * dk:(mp + 1) * dk])

        v = v_ref[...]
        for mp in range(2):
            for r in range(t // strip):
                rows = slice(r * strip, (r + 1) * strip)
                sc = s_sc[mp, rows, :]
                if masked:
                    sc = jnp.where(_iota(sc.shape, 1) <= _iota(sc.shape, 0) + r * strip, sc, NEG_INF)
                m_prev = m_sc[mp, rows, :]
                m_new = jnp.maximum(m_prev, jnp.max(sc, axis=-1, keepdims=True))
                alpha = jnp.exp2(m_prev - m_new)
                p = jnp.exp2(sc - _lane_tile(m_new, t))
                l_sc[mp, rows, :] = alpha * l_sc[mp, rows, :] + jnp.sum(p, axis=-1, keepdims=True)
                a_sc[mp, rows, :] = alpha
                m_sc[mp, rows, :] = m_new
                p_sc[mp, rows, :] = p.astype(BF16)
            acc_sc[mp] = (_lane_tile(a_sc[mp], dv) * acc_sc[mp]
                          + jnp.dot(p_sc[mp], v, preferred_element_type=F32))

    @pl.when(j < i)
    def _():
        step(False)

    @pl.when(j == i)
    def _():
        step(True)
        lam = _lambda(lq1_ref, lk1_ref, lq2_ref, lk2_ref, lam_init)
        o = acc_sc[0] / _lane_tile(l_sc[0], dv) - lam * (acc_sc[1] / _lane_tile(l_sc[1], dv))
        o_ref[...] = _subln(o, g_ref[...], lam_init).astype(o_ref.dtype)


def _prompt_attention(q, k, v, lam_vecs, g, lam_init, heads, dk, dv):
    s_len = q.shape[0]
    t = _tile(s_len, ATTN_TILE)
    nb = s_len // t
    pairs = [(i, j) for i in range(nb) for j in range(i + 1)]
    it = jnp.asarray([p[0] for p in pairs], jnp.int32)
    jt = jnp.asarray([p[1] for p in pairs], jnp.int32)
    vec = pl.BlockSpec((1, dk), lambda h, s, it, jt: (0, 0))
    grid_spec = pltpu.PrefetchScalarGridSpec(
        num_scalar_prefetch=2,
        grid=(heads, len(pairs)),
        in_specs=[pl.BlockSpec((t, 2 * dk), lambda h, s, it, jt: (it[s], h)),
                  pl.BlockSpec((t, 2 * dk), lambda h, s, it, jt: (jt[s], h)),
                  pl.BlockSpec((t, dv), lambda h, s, it, jt: (jt[s], h)),
                  vec, vec, vec, vec,
                  pl.BlockSpec((1, dv), lambda h, s, it, jt: (0, 0))],
        out_specs=pl.BlockSpec((t, dv), lambda h, s, it, jt: (it[s], h)),
        scratch_shapes=[pltpu.VMEM((2, t, t), F32), pltpu.VMEM((2, t, t), BF16), pltpu.VMEM((2, t, LANES), F32),
                        pltpu.VMEM((2, t, LANES), F32), pltpu.VMEM((2, t, LANES), F32),
                        pltpu.VMEM((2, t, dv), F32)],
    )
    return pl.pallas_call(
        functools.partial(_prompt_attn_kernel, lam_init=lam_init, dk=dk, strip=min(ATTN_STRIP, t)),
        grid_spec=grid_spec,
        out_shape=jax.ShapeDtypeStruct((s_len, heads * dv), BF16),
        compiler_params=_cparams(("parallel", "arbitrary")),
        name="prompt_attention",
    )(it, jt, q, k, v, *lam_vecs, g.reshape(1, dv))


def _sample_attn_kernel(pt_ref, q_ref, kn_ref, vn_ref, *rest, lam_init, heads, dk, dv, pages):
    k_refs = rest[:pages]
    v_refs = rest[pages:2 * pages]
    (lq1_ref, lk1_ref, lq2_ref, lk2_ref, g_ref, o_ref, q_sc, bias_sc, m_sc, l_sc, acc_sc) = rest[2 * pages:]
    step_id = pl.program_id(1)
    half = SUBLANES // 2
    rp = heads * half
    page_lanes = bias_sc.shape[1]

    def own_head(shape):
        return _iota(shape, 1) % heads == (_iota(shape, 0) % rp) // half

    @pl.when(step_id == 0)
    def _():
        q = q_ref[0:SUBLANES, :]
        for par in range(2):
            for j in range(heads // 2):
                lo = q[:, (4 * j + par) * dk:(4 * j + par + 1) * dk]
                hi = q[:, (4 * j + 2 + par) * dk:(4 * j + 3 + par) * dk]
                q_sc[par * rp + j * SUBLANES:par * rp + (j + 1) * SUBLANES, :] = lo + pltpu.roll(hi, half, 0)
        bias_sc[...] = jnp.where(own_head(bias_sc.shape), 0.0, NEG_INF)
        m_sc[...] = jnp.full(m_sc.shape, NEG_INF, F32)
        l_sc[...] = jnp.zeros(l_sc.shape, F32)
        acc_sc[...] = jnp.zeros(acc_sc.shape, F32)

    def scores(k_ref, lanes):
        return jnp.concatenate(
            [_dot_nt(q_sc[par * rp:(par + 1) * rp, :], k_ref[pl.ds(par, lanes, stride=2), :]) for par in range(2)],
            axis=0)

    def update(sc, v_list):
        m_prev = m_sc[...]
        m_new = jnp.maximum(m_prev, jnp.max(sc, axis=-1, keepdims=True))
        alpha = jnp.exp2(m_prev - m_new)
        p = jnp.exp2(sc - m_new)
        l_sc[...] = alpha * l_sc[...] + jnp.sum(p, axis=-1, keepdims=True)
        pv, off = None, 0
        for vr in v_list:
            n = vr.shape[0]
            part = _dot(p[:, off:off + n], vr[...])
            pv = part if pv is None else pv + part
            off += n
        acc_sc[...] = alpha * acc_sc[...] + pv
        m_sc[...] = m_new

    bias = bias_sc[...]
    update(jnp.concatenate([scores(kr, page_lanes) + bias for kr in k_refs], axis=1), v_refs)

    @pl.when(step_id == pl.num_programs(1) - 1)
    def _():
        sc = scores(kn_ref, vn_ref.shape[0])
        ok = own_head(sc.shape) & (_iota(sc.shape, 1) // heads <= _iota(sc.shape, 0) % half)
        update(jnp.where(ok, sc, NEG_INF), [vn_ref])
        lam = _lambda(lq1_ref, lk1_ref, lq2_ref, lk2_ref, lam_init)
        accn = acc_sc[...] / l_sc[...]
        y = _subln(accn[:rp] - lam * accn[rp:], g_ref[...], lam_init)
        for j in range(heads // 2):
            y8 = y[j * SUBLANES:(j + 1) * SUBLANES]
            o_ref[:, (2 * j) * dv:(2 * j + 1) * dv] = y8
            o_ref[:, (2 * j + 1) * dv:(2 * j + 2) * dv] = pltpu.roll(y8, half, 0)


def _sample_attention(q, k_new, v_new, cache_k, cache_v, pool_offset, page_table, lam_vecs, g, lam_init, n_new):
    b, pad_t, width = q.shape
    n_pages = page_table.shape[1]
    dk, dv = cache_k.shape[2], cache_v.shape[2]
    maps = width // dk
    heads = maps // 2
    page = cache_k.shape[1] // maps
    half = SUBLANES // 2
    assert n_new <= half and pad_t >= SUBLANES and heads % 2 == 0
    pages = math.gcd(n_pages, PAGES_PER_STEP)
    rows = 2 * heads * half
    vec = pl.BlockSpec((1, dk), lambda bi, s, pt: (0, 0))
    per_row = lambda a: pl.BlockSpec((None,) + a.shape[1:], lambda bi, s, pt: (bi, 0, 0))

    def page_spec(pg, a):
        return pl.BlockSpec((None,) + a.shape[1:], lambda bi, s, pt: (pool_offset + pt[bi, s * pages + pg], 0, 0))

    grid_spec = pltpu.PrefetchScalarGridSpec(
        num_scalar_prefetch=1,
        grid=(b, n_pages // pages),
        in_specs=([per_row(q), per_row(k_new), per_row(v_new)] + [page_spec(pg, cache_k) for pg in range(pages)]
                  + [page_spec(pg, cache_v) for pg in range(pages)]
                  + [vec, vec, vec, vec, pl.BlockSpec((1, dv), lambda bi, s, pt: (0, 0))]),
        out_specs=pl.BlockSpec((None, SUBLANES, heads * dv), lambda bi, s, pt: (bi, 0, 0)),
        scratch_shapes=[pltpu.VMEM((rows, dk), F32), pltpu.VMEM((rows, page * heads), F32),
                        pltpu.VMEM((rows, 1), F32), pltpu.VMEM((rows, 1), F32), pltpu.VMEM((rows, dv), F32)],
    )
    return pl.pallas_call(
        functools.partial(_sample_attn_kernel, lam_init=lam_init, heads=heads, dk=dk, dv=dv, pages=pages),
        grid_spec=grid_spec,
        out_shape=jax.ShapeDtypeStruct((b, SUBLANES, heads * dv), F32),
        compiler_params=_cparams(("parallel", "arbitrary")),
        name="sample_attention",
    )(page_table, q, k_new, v_new, *([cache_k] * pages), *([cache_v] * pages), *lam_vecs, g.reshape(1, dv))


def _segsums(xs, ones_bd):
    parts = []
    for x in xs:
        hi = x.astype(BF16)
        parts += [hi, (x - hi.astype(F32)).astype(BF16)]
    s = jnp.dot(jnp.concatenate(parts, axis=0), ones_bd, preferred_element_type=F32)
    m = xs[0].shape[0]
    return [s[2 * i * m:(2 * i + 1) * m] + s[(2 * i + 1) * m:(2 * i + 2) * m] for i in range(len(xs))]


def _segsum(x, ones_bd):
    return _segsums([x], ones_bd)[0]


def _block_ones(w):
    return jnp.where(_iota((w, w), 0) // RW_N == _iota((w, w), 1) // RW_N, 1.0, 0.0).astype(BF16)


def _rw_prep_kernel(z_ref, zprev_ref, mu_ref, w0_ref, w2_ref, a0_ref, a2_ref, g2_ref, kk_ref, ka_ref,
                    r_out, lw_out, k_out, v_out, kk_out, b_out, g_out, carry_sc, *, mode, period, rw, dd, da, gp):
    z = z_ref[...]
    tm = z.shape[0]
    rolled = pltpu.roll(z, 1, 0)
    row = _iota(z.shape, 0)
    if mode == "carry":
        @pl.when(pl.program_id(0) == 0)
        def _():
            carry_sc[...] = zprev_ref[...]
        zs = jnp.where(row == 0, carry_sc[...], rolled)
        carry_sc[...] = z[tm - 1:tm, :]
    else:
        zs = jnp.where(row % period == 0, zprev_ref[...], rolled)
    zz = z + (zs - z) * mu_ref[...]
    r = zz[:, :rw]
    kr = zz[:, rw:2 * rw]
    v = zz[:, 2 * rw:3 * rw]
    zw = zz[:, 3 * rw:3 * rw + dd]
    za = zz[:, 3 * rw + dd:3 * rw + dd + da]
    zg = zz[:, 3 * rw + dd + da:3 * rw + dd + da + gp]
    u = w0_ref[...] + _dot(jnp.tanh(zw), w2_ref[...])
    w_log = jnp.minimum(u, 0.0) - jnp.log(1.0 + jnp.exp(-jnp.abs(u))) - 0.5
    lw_out[...] = -jnp.exp(w_log)
    a = 1.0 / (1.0 + jnp.exp(-(a0_ref[...] + _dot(za, a2_ref[...]))))
    g_out[...] = _dot(1.0 / (1.0 + jnp.exp(-zg)), g2_ref[...])
    ones_bd = _block_ones(LANES)
    kk = kr * kk_ref[...]
    for c in range(rw // LANES):
        sl = slice(c * LANES, (c + 1) * LANES)
        kc = kk[:, sl]
        kc = kc * lax.rsqrt(jnp.maximum(_segsum(kc * kc, ones_bd), 1e-24))
        kk_out[:, sl] = kc
        b_out[:, sl] = kc * a[:, sl]
    r_out[...] = r
    v_out[...] = v
    k_out[...] = kr * (1.0 + (a - 1.0) * ka_ref[...])


def _rw_prep(z, zprev, prm, mode, period):
    m, zp = z.shape
    rw, dd, da, gp = prm["rw"], prm["dd"], prm["da"], prm["gp"]
    tm = _tile(m, 128)
    row = pl.BlockSpec((tm, zp), lambda i: (i, 0))
    full = lambda a: pl.BlockSpec(a.shape, lambda i: (0,) * a.ndim)
    zprev_spec = full(zprev) if mode == "carry" else row
    params = [prm["mu"], prm["w0"], prm["w2"], prm["a0"], prm["a2"], prm["g2"], prm["k_k"], prm["k_a"]]
    out_spec = pl.BlockSpec((tm, rw), lambda i: (i, 0))
    return pl.pallas_call(
        functools.partial(_rw_prep_kernel, mode=mode, period=period, rw=rw, dd=dd, da=da, gp=gp),
        grid=(m // tm,),
        in_specs=[row, zprev_spec] + [full(p) for p in params],
        out_specs=[out_spec] * 7,
        out_shape=[jax.ShapeDtypeStruct((m, rw), F32)] * 7,
        scratch_shapes=[pltpu.VMEM((1, zp), F32)],
        compiler_params=_cparams(("arbitrary",)),
        name="rwkv_prep",
    )(z, zprev, *params)


def _bd(x, gw):
    head = (_iota(x.shape, 1) % gw) // RW_N
    return jnp.concatenate([jnp.where(head == h, x, 0.0) for h in range(gw // RW_N)], axis=0)


def _rw_scan_kernel(*refs, has_s0, has_cast, t_valid, gw, n_groups):
    if has_cast:
        *refs, cast_in, y_out, s_out, cast_out, sb_sc = refs
        refs = refs + [y_out, s_out, sb_sc]
        cast_out[...] = cast_in[...].astype(cast_out.dtype)
    if has_s0:
        (r_ref, lw_ref, k_ref, v_ref, kk_ref, b_ref, g_ref, rk_ref, lng_ref, lnb_ref, s0_ref,
         y_out, s_out, sb_sc) = refs
    else:
        (r_ref, lw_ref, k_ref, v_ref, kk_ref, b_ref, g_ref, rk_ref, lng_ref, lnb_ref,
         y_out, s_out, sb_sc) = refs
    c = RW_CHUNK
    n = RW_N
    tc = r_ref.shape[0]
    ci = pl.program_id(1)
    last = pl.num_programs(1) - 1

    tri_inc = jnp.where(_iota((c, c), 1) <= _iota((c, c), 0), 1.0, 0.0).astype(BF16)
    s_idx = _iota((c, gw), 1) % n
    t_idx = _iota((c, gw), 0)
    strict = s_idx < t_idx
    incl = s_idx <= t_idx
    eye_w = jnp.where(s_idx == t_idx, 1.0, 0.0)
    gi0, gi1 = _iota((gw, gw), 0), _iota((gw, gw), 1)
    blk = gi0 // n == gi1 // n
    eye_g = jnp.where(gi0 == gi1, 1.0, 0.0)
    ones_bd = jnp.where(blk, 1.0, 0.0).astype(BF16)
    sel = jnp.where(_iota((gw, n), 0) % n == _iota((gw, n), 1), 1.0, 0.0).astype(BF16)
    sel_t = jnp.where(_iota((n, gw), 1) % n == _iota((n, gw), 0), 1.0, 0.0).astype(BF16)
    valid = _iota((tc, gw), 0) < t_valid

    def load(ref, sl, fill=0.0):
        x = ref[:, sl]
        if t_valid < tc:
            x = jnp.where(valid, x, fill)
        if tc < c:
            x = jnp.concatenate([x, jnp.full((c - tc, gw), fill, F32)], axis=0)
        return x

    @pl.when(ci == 0)
    def _():
        for gi in range(n_groups):
            if has_s0:
                sb_sc[gi] = jnp.where(blk, _dot_sel_r(s0_ref[gi * gw:(gi + 1) * gw, :], sel_t), 0.0)
            else:
                sb_sc[gi] = jnp.zeros((gw, gw), F32)

    def group_chain(gi):
        sl = slice(gi * gw, (gi + 1) * gw)
        r, lw, k, v, kk, b = (load(ref, sl) for ref in (r_ref, lw_ref, k_ref, v_ref, kk_ref, b_ref))
        cl = _dot_sel(tri_inc, lw)
        yield
        cend = cl[c - 1:c, :]
        kt = kk * jnp.exp(cl - lw)
        rt = r * jnp.exp(cl)
        g_inv = jnp.exp(-cl)
        g_rat = jnp.exp(cend - cl)
        ktil, btil = k * g_inv, b * g_inv
        khat, bhat = k * g_rat, b * g_rat
        la = _dot_nt(jnp.concatenate([kt, rt], axis=0),
                     jnp.concatenate([_bd(btil, gw), _bd(ktil, gw)], axis=0))
        yield
        lb = jnp.where(strict, la[:c, :gw], 0.0)
        lk = jnp.where(strict, la[:c, gw:], 0.0)
        ab = jnp.where(incl, la[c:, :gw], 0.0)
        ak = jnp.where(incl, la[c:, gw:], 0.0)
        lav = _dot(jnp.concatenate([lk, ak], axis=0), _bd(v, gw))
        lkv, y0 = lav[:c], lav[c:]
        qt = _dot_tn(v, khat)
        x = eye_w - lb
        rounds = max(0, math.ceil(math.log2(t_valid)) - 1)
        if rounds:
            p = _dot(lb, _bd(lb, gw))
            yield
        for rnd in range(rounds):
            if rnd == rounds - 1:
                x = x + _dot(x, _bd(p, gw))
            else:
                px = _dot(jnp.concatenate([p, x], axis=0), _bd(p, gw))
                p, x = px[:c], x + px[c:]
            yield
        tw = _dot(x, _bd(jnp.concatenate([kt, lkv], axis=1), gw))
        yield
        abw = _dot(ab, _bd(tw, gw))
        tn = _dot_tn(tw, bhat)
        yield
        rr = rt - abw[:, :gw]
        y0 = y0 - abw[:, gw:]
        pt = eye_g * jnp.exp(cend) - jnp.where(blk, tn[:gw], 0.0)
        qt = jnp.where(blk, qt - tn[gw:], 0.0)
        s_old = sb_sc[gi]
        y = _dot_nt(rr, s_old) + y0
        sb_sc[gi] = _dot(s_old, pt) + qt
        yield
        y_sum, rk_sum = _segsums([y, r * k * rk_ref[:, sl]], ones_bd)
        bonus = rk_sum * v
        yield
        d = y - y_sum * (1.0 / n)
        var = _segsum(d * d, ones_bd) * (1.0 / n)
        yield
        yn = d * lax.rsqrt(var + RW_LN_EPS) * lng_ref[:, sl] + lnb_ref[:, sl]
        out = (yn + bonus)[:tc] * g_ref[:, sl]
        y_out[:, sl] = out.astype(y_out.dtype)

    chains = [group_chain(gi) for gi in range(n_groups)]
    while chains:
        chains = [ch for ch in chains if next(ch, "done") != "done"]

    @pl.when(ci == last)
    def _():
        for gi in range(n_groups):
            s_out[gi * gw:(gi + 1) * gw, :] = _dot_sel_r(sb_sc[gi], sel)


def _rw_scan(r, lw, k, v, kk, b, g, prm, s0, n_seq, tc, t_valid, cast_src=None):
    m, rw = r.shape
    n_chunks = m // (n_seq * tc)
    gw = math.gcd(rw, RW_GROUP)
    n_groups = rw // gw
    row = pl.BlockSpec((tc, rw), lambda s, ci: (s * n_chunks + ci, 0))
    vec = pl.BlockSpec((1, rw), lambda s, ci: (0, 0))
    state = pl.BlockSpec((None, rw, RW_N), lambda s, ci: (s, 0, 0))
    has_s0 = s0 is not None
    ins = [r, lw, k, v, kk, b, g, prm["r_k"], prm["ln_g"], prm["ln_b"]] + ([s0] if has_s0 else [])
    in_specs = [row] * 7 + [vec] * 3 + ([state] if has_s0 else [])
    out_specs = [row, state]
    out_shape = [jax.ShapeDtypeStruct((m, rw), BF16), jax.ShapeDtypeStruct((n_seq, rw, RW_N), F32)]
    if cast_src is not None:
        src, layer = cast_src
        steps = n_seq * n_chunks
        assert src.shape[1] % steps == 0
        slab = src.shape[1] // steps
        in_specs.append(pl.BlockSpec((None, slab, src.shape[2]),
                                     lambda s, ci, layer=layer: (layer, s * n_chunks + ci, 0)))
        ins.append(src)
        out_specs.append(pl.BlockSpec((slab, src.shape[2]), lambda s, ci: (s * n_chunks + ci, 0)))
        out_shape.append(jax.ShapeDtypeStruct(src.shape[1:], BF16))
    out = pl.pallas_call(
        functools.partial(_rw_scan_kernel, has_s0=has_s0, has_cast=cast_src is not None, t_valid=t_valid, gw=gw,
                          n_groups=n_groups),
        grid=(n_seq, n_chunks),
        in_specs=in_specs,
        out_specs=out_specs,
        out_shape=out_shape,
        scratch_shapes=[pltpu.VMEM((n_groups, gw, gw), F32)],
        compiler_params=_cparams(("arbitrary", "arbitrary")),
        name="rwkv_scan",
    )(*ins)
    return (out[0], out[1], out[2][None]) if cast_src is not None else (out[0], out[1])


def _rope_tables(pos, dk):
    half = dk // 2
    inv = ROPE_THETA ** (-jnp.arange(half, dtype=F32) / half)
    ang = pos.astype(F32)[:, None] * inv[None, :]
    cos, sin = jnp.cos(ang), jnp.sin(ang)
    return jnp.concatenate([cos, cos], axis=-1), jnp.concatenate([-sin, sin], axis=-1)


def _prep_weights(i, dims, w_in, rw_mu, rw_w0, rw_w2, rw_a0, rw_a2, rw_g2, rw_k_k, rw_k_a, rw_r_k, rw_ln_g,
                  rw_ln_b):
    qw, da_w, rw, dd, da, dg, gp, zp = (dims[n] for n in ("qw", "da_w", "rw", "dd", "da", "dg", "gp", "zp"))
    rw_cols = 3 * rw + dd + da + dg
    w = w_in[i].astype(BF16)[None]
    prm = {
        "rw": rw, "dd": dd, "da": da, "gp": gp,
        "mu": jnp.pad(rw_mu[i], (0, zp - rw_cols)).reshape(1, zp),
        "w0": rw_w0[i].reshape(1, rw), "w2": rw_w2[i].astype(BF16),
        "a0": rw_a0[i].reshape(1, rw), "a2": rw_a2[i].astype(BF16),
        "g2": jnp.pad(rw_g2[i], ((0, gp - dg), (0, 0))).astype(BF16),
        "k_k": rw_k_k[i].reshape(1, rw), "k_a": rw_k_a[i].reshape(1, rw),
        "r_k": rw_r_k[i].reshape(1, rw), "ln_g": rw_ln_g[i].reshape(1, rw), "ln_b": rw_ln_b[i].reshape(1, rw),
    }
    return w, prm


def _tail(i, xs, os_, rwos, pes, w_out_b, w_up_b, w_down, w_gate_b, w_ple_proj, gains):
    g_post_mix, g_pre_ffn, g_post_ffn, g_pre_ple, g_post_ple = gains
    d = xs[0].shape[1]
    d_ff = w_up_b.shape[2]
    half = os_[0].shape[1]
    assert rwos[0].shape[1] == half and w_out_b.shape[1] == 2 * half
    tn = _tile(d, MM_TILE)
    mix = _dense(_mm2_fn, [(os_[0], os_[1], False), (rwos[0], rwos[1], False)],
                 [(w_out_b, 0, half, 0, 0), (w_out_b, 0, half, 1, 0)], [(F32, None, F32)], n=d, tn=tn,
                 tm=MM_TILE // 2, name="out_proj")
    x, h = zip(*(_norm_residual(f[0], xx, g_post_mix, g_pre_ffn) for f, xx in zip(mix, xs)))
    tk = _tile(d_ff, FFN_TK)
    (u,), _, w_down_b = _dense(_ffn_up_fn, [(h[0], None, False)], [(w_up_b, 0, d, 0, 0)], [(BF16, None, None)],
                               n=d_ff, tn=_tile(d_ff, MM_TILE), cast_src=(w_down, i), name="ffn_up")
    (u_s,), _ = _dense(_ffn_up_fn, [(h[1], None, False)], [(w_up_b, 0, d, 0, 0)], [(BF16, None, None)],
                       n=d_ff, tn=_tile(d_ff, MM_TILE), name="ffn_up")
    f = [_dense(_mm_acc_fn, [(uu, None, True)], [(w_down_b, 0, tk, 0, 0)], [(F32, None, None)],
                n=d, tn=tn, k_steps=d_ff // tk, name="ffn_down")[0] for uu in (u, u_s)]
    x, h = zip(*(_norm_residual(ff[0], xx, g_post_ffn, g_pre_ple) for ff, xx in zip(f, x)))
    e = _dense(_ple_fn, [(h[0], h[1], False), (pes[0], pes[1], False)],
               [(w_gate_b, 0, d, 0, 0), (w_ple_proj.astype(BF16), i, pes[0].shape[1], 0, 0)], [(F32, None, F32)],
               n=d, tn=tn, tm=MM_TILE // 2, name="ple")
    return tuple(_norm_residual(ee[0], xx, g_post_ple, None)[0] for ee, xx in zip(e, x))


def kernel(x_prompt, x_sample, cache_k, cache_v, state_wkv, state_shift, page_table, p_prompt, p_sample, g_pre_mix, w_in, lam_q1, lam_k1, lam_q2, lam_k2, da_subln_g, rw_mu, rw_w0, rw_w2, rw_a0, rw_a2, rw_g2, rw_k_k, rw_k_a, rw_r_k, rw_ln_g, rw_ln_b, w_out, g_post_mix, g_pre_ffn, w_up, w_down, g_post_ffn, g_pre_ple, w_ple_gate, w_ple_proj, g_post_ple):
    batch, seq, d = x_prompt.shape
    db, dseq, _ = x_sample.shape
    depth, n_pool, page, maps, dk = cache_k.shape
    heads, dv = cache_v.shape[3], cache_v.shape[4]
    rwh, rwn = rw_r_k.shape[1], rw_r_k.shape[2]
    assert maps == 2 * heads and dk == LANES and rwn == RW_N and seq % RW_CHUNK == 0
    qw, da_w, rw = maps * dk, heads * dv, rwh * rwn
    dd, da, dg = rw_w2.shape[1], rw_a2.shape[1], rw_g2.shape[1]
    assert dd % LANES == 0 and da % LANES == 0 and rw % LANES == 0
    gp = _round_up(dg, LANES)
    rw_cols = 3 * rw + dd + da + dg
    zp = _round_up(3 * rw + dd + da + gp, min(MM_TILE, _round_up(rw_cols, LANES)))
    dims = dict(qw=qw, da_w=da_w, rw=rw, dd=dd, da=da, dg=dg, gp=gp, zp=zp)
    n_pages = page_table.shape[1]
    past_len = n_pages * page
    pad_t = _round_up(dseq, SUBLANES)
    pad_a = _round_up(dseq, 2 * SUBLANES)
    scale = dk ** -0.5

    cos_p, sin_p = _rope_tables(jnp.tile(jnp.arange(seq, dtype=jnp.int32), batch), dk)
    cos_s, sin_s = _rope_tables(jnp.tile(past_len + jnp.arange(dseq, dtype=jnp.int32), db), dk)

    xp = x_prompt.reshape(batch * seq, d)
    xs = x_sample.reshape(db * dseq, d)
    outs = [[] for _ in range(8)]
    for i in range(depth):
        lam_init = 0.8 - 0.6 * math.exp(-0.3 * i)
        w, prm = _prep_weights(i, dims, w_in, rw_mu, rw_w0, rw_w2, rw_a0, rw_a2, rw_g2, rw_k_k, rw_k_a, rw_r_k,
                               rw_ln_g, rw_ln_b)
        lam_vecs = [v[i].reshape(1, dk) for v in (lam_q1, lam_k1, lam_q2, lam_k2)]
        gains = (g_post_mix[i], g_pre_ffn[i], g_post_ffn[i], g_pre_ple[i], g_post_ple[i])

        h = _rmsnorm(xp, g_pre_mix[i], NORM_EPS)
        h_s = _rmsnorm(xs, g_pre_mix[i], NORM_EPS)
        roped = [(h, h_s, False), (cos_p, cos_s, False), (sin_p, sin_s, False)]
        tn_q, tn_z = _tile(qw, MM_TILE), _tile(zp, MM_TILE)
        assert qw % tn_q == 0 and (2 * qw) % da_w == 0 and (2 * qw + da_w) % tn_z == 0
        (q,), (q_s,), w_gate_b = _dense(
            functools.partial(_proj_fn, rope=True, scale=scale * math.log2(math.e), hw=dk),
            roped, [(w, 0, d, 0, 0)], [(BF16, None, F32)], n=qw, tn=tn_q, tm=MM_TILE // 2,
            cast_src=(w_ple_gate, i), name="proj_q")
        (k32, k16), (k32_s,), w_out_b = _dense(
            functools.partial(_proj_fn, rope=True, scale=1.0, hw=dk),
            roped, [(w, 0, d, 0, qw // tn_q)], [(F32, dk, F32), (BF16, None, None)], n=qw, tn=tn_q,
            tm=MM_TILE // 2, cast_src=(w_out, i), name="proj_k")
        v_fn = functools.partial(_proj_fn, rope=False, scale=1.0, hw=dv)
        v_w = [(w, 0, d, 0, 2 * qw // da_w)]
        (v32, v16), _ = _dense(v_fn, [(h, None, False)], v_w, [(F32, dv, None), (BF16, None, None)], n=da_w,
                               tn=da_w, tm=MM_TILE // 2, name="proj_v")
        (v32_s,), _ = _dense(v_fn, [(h_s, None, False)], v_w, [(F32, None, None)], n=da_w, tn=da_w, name="proj_v")
        (z,), (z_s,) = _dense(functools.partial(_proj_fn, rope=False, scale=1.0, hw=tn_z, n_valid=rw_cols),
                              [(h, h_s, False)],
                              [(w, 0, d, 0, (2 * qw + da_w) // tn_z)], [(F32, None, F32)], n=zp, tn=tn_z,
                              name="proj_z")

        o_rows, rw_rows, wkv_rows = [], [], []
        for bi in range(batch):
            rs = slice(bi * seq, (bi + 1) * seq)
            o_rows.append(_prompt_attention(q[rs], k16[rs], v16[rs], lam_vecs, da_subln_g[i], lam_init,
                                            heads, dk, dv))
            mixer_in = _rw_prep(z[rs], jnp.zeros((1, zp), F32), prm, "carry", 0)
            if bi == 0:
                rwo, s_fin, w_up_b = _rw_scan(*mixer_in, prm, None, 1, RW_CHUNK, RW_CHUNK, cast_src=(w_up, i))
            else:
                rwo, s_fin = _rw_scan(*mixer_in, prm, None, 1, RW_CHUNK, RW_CHUNK)
            rw_rows.append(rwo)
            wkv_rows.append(s_fin.reshape(rwh, rwn, rwn))
        o = o_rows[0] if batch == 1 else jnp.concatenate(o_rows, axis=0)
        rwo = rw_rows[0] if batch == 1 else jnp.concatenate(rw_rows, axis=0)
        outs[0].append(k32.reshape(batch, seq, maps, dk))
        outs[1].append(v32.reshape(batch, seq, heads, dv))
        outs[4].append(jnp.stack(wkv_rows, 0))
        outs[6].append(z.reshape(batch, seq, zp)[:, -1, :rw_cols])

        pad3 = lambda a, t: jnp.pad(a.reshape(db, dseq, -1), ((0, 0), (0, t - dseq), (0, 0)))
        o_s = _sample_attention(pad3(q_s, pad_a), pad3(k32_s, pad_a).reshape(db, pad_a * maps, dk),
                                pad3(v32_s, pad_a).reshape(db, pad_a * heads, dv),
                                cache_k.reshape(depth * n_pool, page * maps, dk),
                                cache_v.reshape(depth * n_pool, page * heads, dv), i * n_pool, page_table,
                                lam_vecs, da_subln_g[i], lam_init, dseq)
        o_s = o_s[:, :dseq].reshape(db * dseq, da_w).astype(BF16)
        zprev = jnp.zeros((db, pad_t, zp), F32).at[:, 0, :rw_cols].set(state_shift[i])
        mixer_in = _rw_prep(pad3(z_s, pad_t).reshape(db * pad_t, zp), zprev.reshape(db * pad_t, zp), prm, "rows",
                            pad_t)
        rwo_s, s_fin = _rw_scan(*mixer_in, prm, state_wkv[i].reshape(db, rw, rwn), db, pad_t, dseq)
        rwo_s = rwo_s.reshape(db, pad_t, rw)[:, :dseq].reshape(db * dseq, rw)
        outs[2].append(k32_s.reshape(db, dseq, maps, dk))
        outs[3].append(v32_s.reshape(db, dseq, heads, dv))
        outs[5].append(s_fin.reshape(db, rwh, rwn, rwn))
        outs[7].append(z_s.reshape(db, dseq, zp)[:, -1, :rw_cols])

        xp, xs = _tail(i, (xp, xs), (o, o_s), (rwo, rwo_s),
                       (p_prompt[i].reshape(batch * seq, -1), p_sample[i].reshape(db * dseq, -1)),
                       w_out_b, w_up_b, w_down, w_gate_b, w_ple_proj, gains)

    st =[jnp.stack(o, 0) for o in outs]
    return (xp.reshape(batch, seq, d), xs.reshape(db, dseq, d), st[0], st[1], st[2], st[3], st[4], st[5],
            st[6], st[7])
```

```python
import functools
import math

import jax
import jax.numpy as jnp
from jax import lax
from jax.experimental import pallas as pl
from jax.experimental.pallas import tpu as pltpu

F32 = jnp.float32
BF16 = jnp.bfloat16

LANES = 128
SUBLANES = 8
VMEM_LIMIT_BYTES = 56 * 1024 * 1024

ROPE_THETA = 10000.0
NORM_EPS = 1e-6
SUBLN_EPS = 1e-5
RW_LN_EPS = 64e-5
NEG_INF = -1e30

RW_N = 64
RW_CHUNK = 64
RW_GROUP = 256
MM_TILE = 1024
FFN_TK = 4096
ATTN_TILE = 1024
ATTN_STRIP = 32
PAGES_PER_STEP = 8


def _cparams(sem):
    return pltpu.CompilerParams(dimension_semantics=sem, vmem_limit_bytes=VMEM_LIMIT_BYTES)


def _round_up(x, m):
    return (x + m - 1) // m * m


def _tile(n, pref):
    t = min(n, pref)
    assert n % t == 0, (n, t)
    return t


def _dot(a, b):
    return jnp.dot(a.astype(BF16), b.astype(BF16), preferred_element_type=F32)


def _dot_nt(a, b):
    return lax.dot_general(a.astype(BF16), b.astype(BF16), (((1,), (1,)), ((), ())),
                           preferred_element_type=F32)


def _dot_tn(a, b):
    return lax.dot_general(a.astype(BF16), b.astype(BF16), (((0,), (0,)), ((), ())),
                           preferred_element_type=F32)


def _split3(x):
    hi = x.astype(BF16)
    r1 = x - hi.astype(F32)
    mid = r1.astype(BF16)
    lo = (r1 - mid.astype(F32)).astype(BF16)
    return hi, mid, lo


def _dot_sel(sel, x):
    hi, mid, lo = _split3(x)
    n = x.shape[1]
    y = jnp.dot(sel, jnp.concatenate([hi, mid, lo], axis=1), preferred_element_type=F32)
    return y[:, :n] + y[:, n:2 * n] + y[:, 2 * n:]


def _dot_sel_r(x, sel):
    hi, mid, lo = _split3(x)
    m = x.shape[0]
    y = jnp.dot(jnp.concatenate([hi, mid, lo], axis=0), sel, preferred_element_type=F32)
    return y[:m] + y[m:2 * m] + y[2 * m:]


def _iota(shape, dim):
    return lax.broadcasted_iota(jnp.int32, shape, dim)


def _rmsnorm_kernel(x_ref, g_ref, o_ref, *, eps):
    x = x_ref[...]
    y = x * lax.rsqrt(jnp.mean(x * x, axis=-1, keepdims=True) + eps)
    o_ref[...] = (y * g_ref[...]).astype(o_ref.dtype)


def _rmsnorm(x, g, eps):
    m, d = x.shape
    tm = _tile(m, 256)
    return pl.pallas_call(
        functools.partial(_rmsnorm_kernel, eps=eps),
        grid=(m // tm,),
        in_specs=[pl.BlockSpec((tm, d), lambda i: (i, 0)), pl.BlockSpec((1, d), lambda i: (0, 0))],
        out_specs=pl.BlockSpec((tm, d), lambda i: (i, 0)),
        out_shape=jax.ShapeDtypeStruct((m, d), BF16),
        compiler_params=_cparams(("parallel",)),
        name="rmsnorm",
    )(x, g.reshape(1, d))


def _norm_residual_kernel(f_ref, x_ref, gpost_ref, *rest, has_next):
    f = f_ref[...]
    y = f * lax.rsqrt(jnp.mean(f * f, axis=-1, keepdims=True) + NORM_EPS) * gpost_ref[...]
    xn = x_ref[...] + y
    if has_next:
        gnext_ref, xo_ref, ho_ref = rest
        xo_ref[...] = xn
        h = xn * lax.rsqrt(jnp.mean(xn * xn, axis=-1, keepdims=True) + NORM_EPS) * gnext_ref[...]
        ho_ref[...] = h.astype(ho_ref.dtype)
    else:
        (xo_ref,) = rest
        xo_ref[...] = xn


def _norm_residual(f, x, g_post, g_next):
    m, d = x.shape
    tm = _tile(m, 256)
    row = pl.BlockSpec((tm, d), lambda i: (i, 0))
    vec = pl.BlockSpec((1, d), lambda i: (0, 0))
    has_next = g_next is not None
    ins = [f, x, g_post.reshape(1, d)] + ([g_next.reshape(1, d)] if has_next else [])
    out = pl.pallas_call(
        functools.partial(_norm_residual_kernel, has_next=has_next),
        grid=(m // tm,),
        in_specs=[row, row, vec] + ([vec] if has_next else []),
        out_specs=[row, row] if has_next else [row],
        out_shape=([jax.ShapeDtypeStruct((m, d), F32), jax.ShapeDtypeStruct((m, d), BF16)]
                   if has_next else [jax.ShapeDtypeStruct((m, d), F32)]),
        compiler_params=_cparams(("parallel",)),
        name="norm_residual",
    )(*ins)
    return (out[0], out[1]) if has_next else (out[0], None)


def _dense_kernel(*refs, fn, n_rows, n_w, n_main_out, has_side, has_cast):
    refs = list(refs)
    if has_cast:
        cast_in, cast_out = refs.pop(n_rows * (2 if has_side else 1) + n_w), refs.pop()
        cast_out[...] = cast_in[...].astype(cast_out.dtype)
    main_rows = refs[:n_rows]
    side_rows = refs[n_rows:2 * n_rows] if has_side else []
    pos = n_rows * (2 if has_side else 1)
    w_refs = refs[pos:pos + n_w]
    main_outs = refs[pos + n_w:pos + n_w + n_main_out]
    side_outs = refs[pos + n_w + n_main_out:]
    wv = [w[...].astype(BF16) for w in w_refs]
    fn(main_rows, wv, main_outs)
    if has_side:
        @pl.when(pl.program_id(0) == 0)
        def _():
            fn(side_rows, wv, side_outs)


def _dense(fn, row_ins, w_ins, out_defs, *, n, tn, tm=MM_TILE, k_steps=1, cast_src=None, name):
    m = row_ins[0][0].shape[0]
    has_side = row_ins[0][1] is not None
    tm = _tile(m, tm)
    nj = n // tn
    w_mode = dict(pipeline_mode=pl.Buffered(1)) if nj == 1 and k_steps == 1 else {}
    in_specs, ins = [], []
    for which in ((0, 1) if has_side else (0,)):
        for entry in row_ins:
            a, k_tiled = entry[which], entry[2]
            rows = tm if which == 0 else a.shape[0]
            width = a.shape[1] // k_steps if k_tiled else a.shape[1]
            if which == 0:
                imap = (lambda i, j, kk: (i, kk)) if k_tiled else (lambda i, j, kk: (i, 0))
            else:
                imap = (lambda i, j, kk: (0, kk)) if k_tiled else (lambda i, j, kk: (0, 0))
            in_specs.append(pl.BlockSpec((rows, width), imap))
            ins.append(a)
    for w, layer, k_rows, rb, cb in w_ins:
        in_specs.append(pl.BlockSpec((None, k_rows, tn),
                                     lambda i, j, kk, layer=layer, rb=rb, cb=cb: (layer, rb + kk, cb + j), **w_mode))
        ins.append(w)
    out_specs, out_shape = [], []
    for dt, hw, _ in out_defs:
        if hw is None:
            out_specs.append(pl.BlockSpec((tm, tn), lambda i, j, kk: (i, j)))
            out_shape.append(jax.ShapeDtypeStruct((m, n), dt))
        else:
            out_specs.append(pl.BlockSpec((tm, tn // hw, hw), lambda i, j, kk: (i, j, 0)))
            out_shape.append(jax.ShapeDtypeStruct((m, n // hw, hw), dt))
    n_main_out = len(out_specs)
    if has_side:
        ms = row_ins[0][1].shape[0]
        for _, _, side_dt in out_defs:
            if side_dt is not None:
                out_specs.append(pl.BlockSpec((ms, tn), lambda i, j, kk: (0, jnp.where(i == 0, j, nj - 1))))
                out_shape.append(jax.ShapeDtypeStruct((ms, n), side_dt))
    if cast_src is not None:
        src, layer = cast_src
        steps = (m // tm) * nj * k_steps
        assert src.shape[1] % steps == 0
        slab = src.shape[1] // steps
        in_specs.append(pl.BlockSpec((None, slab, src.shape[2]),
                                     lambda i, j, kk, layer=layer: (layer, (i * nj + j) * k_steps + kk, 0)))
        ins.append(src)
        out_specs.append(pl.BlockSpec((slab, src.shape[2]), lambda i, j, kk: ((i * nj + j) * k_steps + kk, 0)))
        out_shape.append(jax.ShapeDtypeStruct(src.shape[1:], BF16))
    out = pl.pallas_call(
        functools.partial(_dense_kernel, fn=fn, n_rows=len(row_ins), n_w=len(w_ins), n_main_out=n_main_out,
                          has_side=has_side, has_cast=cast_src is not None),
        grid=(m // tm, nj, k_steps),
        in_specs=in_specs,
        out_specs=out_specs,
        out_shape=out_shape,
        compiler_params=_cparams(("arbitrary", "arbitrary", "arbitrary")),
        name=name,
    )(*ins)
    if cast_src is not None:
        return out[:n_main_out], out[n_main_out:-1], out[-1][None]
    return out[:n_main_out], out[n_main_out:]


def _proj_fn(rows, wv, outs, *, rope, scale, hw, n_valid=None):
    acc = jnp.dot(rows[0][...], wv[0], preferred_element_type=F32)
    if n_valid is not None:
        col = pl.program_id(1) * acc.shape[1] + _iota(acc.shape, 1)
        acc = jnp.where(col < n_valid, acc, 0.0)
    if rope:
        cos = rows[1][...]
        sin = rows[2][...]
    for g in range(acc.shape[1] // hw):
        x = acc[:, g * hw:(g + 1) * hw]
        if rope:
            x = x * cos + pltpu.roll(x, hw // 2, 1) * sin
        if scale != 1.0:
            x = x * scale
        for o in outs:
            if len(o.shape) == 3:
                o[:, g, :] = x.astype(o.dtype)
            else:
                o[:, g * hw:(g + 1) * hw] = x.astype(o.dtype)


def _mm2_fn(rows, wv, outs):
    outs[0][...] = (jnp.dot(rows[0][...], wv[0], preferred_element_type=F32)
                    + jnp.dot(rows[1][...], wv[1], preferred_element_type=F32))


def _ffn_up_fn(rows, wv, outs):
    u = jnp.maximum(jnp.dot(rows[0][...], wv[0], preferred_element_type=F32), 0.0)
    outs[0][...] = (u * u).astype(outs[0].dtype)


def _mm_acc_fn(rows, wv, outs):
    part = jnp.dot(rows[0][...], wv[0], preferred_element_type=F32)

    @pl.when(pl.program_id(2) == 0)
    def _():
        outs[0][...] = part

    @pl.when(pl.program_id(2) != 0)
    def _():
        outs[0][...] += part


def _ple_fn(rows, wv, outs):
    gate = jnp.dot(rows[0][...], wv[0], preferred_element_type=F32)
    gate = 1.0 / (1.0 + jnp.exp(-gate))
    proj = jnp.dot(rows[1][...].astype(BF16), wv[1], preferred_element_type=F32)
    outs[0][...] = gate * proj


def _lambda(lq1_ref, lk1_ref, lq2_ref, lk2_ref, lam_init):
    s1 = jnp.sum(lq1_ref[...] * lk1_ref[...], axis=-1, keepdims=True)
    s2 = jnp.sum(lq2_ref[...] * lk2_ref[...], axis=-1, keepdims=True)
    return jnp.exp(s1) - jnp.exp(s2) + lam_init


def _subln(o, g, lam_init):
    y = o * lax.rsqrt(jnp.mean(o * o, axis=-1, keepdims=True) + SUBLN_EPS)
    return y * g * (1.0 - lam_init)


def _lane_tile(x, width):
    return jnp.concatenate([x] * (width // LANES), axis=1)


def _prompt_attn_kernel(it_ref, jt_ref, q_ref, k_ref, v_ref, lq1_ref, lk1_ref, lq2_ref, lk2_ref, g_ref,
                        o_ref, s_sc, p_sc, a_sc, m_sc, l_sc, acc_sc, *, lam_init, dk, strip):
    s = pl.program_id(1)
    i = it_ref[s]
    j = jt_ref[s]
    t = q_ref.shape[0]
    dv = v_ref.shape[1]

    @pl.when(j == 0)
    def _():
        m_sc[...] = jnp.full(m_sc.shape, NEG_INF, F32)
        l_sc[...] = jnp.zeros(l_sc.shape, F32)
        acc_sc[...] = jnp.zeros(acc_sc.shape, F32)

    def step(masked):
        for mp in range(2):
            s_sc[mp] = _dot_nt(q_ref[:, mp * dk:(mp + 1) * dk], k_ref[:, mp * dk:(mp + 1) * dk])

        v = v_ref[...]
        for mp in range(2):
            for r in range(t // strip):
                rows = slice(r * strip, (r + 1) * strip)
                sc = s_sc[mp, rows, :]
                if masked:
                    sc = jnp.where(_iota(sc.shape, 1) <= _iota(sc.shape, 0) + r * strip, sc, NEG_INF)
                m_prev = m_sc[mp, rows, :]
                m_new = jnp.maximum(m_prev, jnp.max(sc, axis=-1, keepdims=True))
                alpha = jnp.exp2(m_prev - m_new)
                p = jnp.exp2(sc - _lane_tile(m_new, t))
                l_sc[mp, rows, :] = alpha * l_sc[mp, rows, :] + jnp.sum(p, axis=-1, keepdims=True)
                a_sc[mp, rows, :] = alpha
                m_sc[mp, rows, :] = m_new
                p_sc[mp, rows, :] = p.astype(BF16)
            acc_sc[mp] = (_lane_tile(a_sc[mp], dv) * acc_sc[mp]
                          + jnp.dot(p_sc[mp], v, preferred_element_type=F32))

    @pl.when(j < i)
    def _():
        step(False)

    @pl.when(j == i)
    def _():
        step(True)
        lam = _lambda(lq1_ref, lk1_ref, lq2_ref, lk2_ref, lam_init)
        o = acc_sc[0] / _lane_tile(l_sc[0], dv) - lam * (acc_sc[1] / _lane_tile(l_sc[1], dv))
        o_ref[...] = _subln(o, g_ref[...], lam_init).astype(o_ref.dtype)


def _prompt_attention(q, k, v, lam_vecs, g, lam_init, heads, dk, dv):
    s_len = q.shape[0]
    t = _tile(s_len, ATTN_TILE)
    nb = s_len // t
    pairs = [(i, j) for i in range(nb) for j in range(i + 1)]
    it = jnp.asarray([p[0] for p in pairs], jnp.int32)
    jt = jnp.asarray([p[1] for p in pairs], jnp.int32)
    vec = pl.BlockSpec((1, dk), lambda h, s, it, jt: (0, 0))
    grid_spec = pltpu.PrefetchScalarGridSpec(
        num_scalar_prefetch=2,
        grid=(heads, len(pairs)),
        in_specs=[pl.BlockSpec((t, 2 * dk), lambda h, s, it, jt: (it[s], h)),
                  pl.BlockSpec((t, 2 * dk), lambda h, s, it, jt: (jt[s], h)),
                  pl.BlockSpec((t, dv), lambda h, s, it, jt: (jt[s], h)),
                  vec, vec, vec, vec,
                  pl.BlockSpec((1, dv), lambda h, s, it, jt: (0, 0))],
        out_specs=pl.BlockSpec((t, dv), lambda h, s, it, jt: (it[s], h)),
        scratch_shapes=[pltpu.VMEM((2, t, t), F32), pltpu.VMEM((2, t, t), BF16), pltpu.VMEM((2, t, LANES), F32),
                        pltpu.VMEM((2, t, LANES), F32), pltpu.VMEM((2, t, LANES), F32),
                        pltpu.VMEM((2, t, dv), F32)],
    )
    return pl.pallas_call(
        functools.partial(_prompt_attn_kernel, lam_init=lam_init, dk=dk, strip=min(ATTN_STRIP, t)),
        grid_spec=grid_spec,
        out_shape=jax.ShapeDtypeStruct((s_len, heads * dv), BF16),
        compiler_params=_cparams(("parallel", "arbitrary")),
        name="prompt_attention",
    )(it, jt, q, k, v, *lam_vecs, g.reshape(1, dv))


def _sample_attn_kernel(pt_ref, q_ref, kn_ref, vn_ref, *rest, lam_init, heads, dk, dv, pages):
    k_refs = rest[:pages]
    v_refs = rest[pages:2 * pages]
    (lq1_ref, lk1_ref, lq2_ref, lk2_ref, g_ref, o_ref, q_sc, bias_sc, m_sc, l_sc, acc_sc) = rest[2 * pages:]
    step_id = pl.program_id(1)
    half = SUBLANES // 2
    rp = heads * half
    page_lanes = bias_sc.shape[1]

    def own_head(shape):
        return _iota(shape, 1) % heads == (_iota(shape, 0) % rp) // half

    @pl.when(step_id == 0)
    def _():
        q = q_ref[0:SUBLANES, :]
        for par in range(2):
            for j in range(heads // 2):
                lo = q[:, (4 * j + par) * dk:(4 * j + par + 1) * dk]
                hi = q[:, (4 * j + 2 + par) * dk:(4 * j + 3 + par) * dk]
                q_sc[par * rp + j * SUBLANES:par * rp + (j + 1) * SUBLANES, :] = lo + pltpu.roll(hi, half, 0)
        bias_sc[...] = jnp.where(own_head(bias_sc.shape), 0.0, NEG_INF)
        m_sc[...] = jnp.full(m_sc.shape, NEG_INF, F32)
        l_sc[...] = jnp.zeros(l_sc.shape, F32)
        acc_sc[...] = jnp.zeros(acc_sc.shape, F32)

    def scores(k_ref, lanes):
        return jnp.concatenate(
            [_dot_nt(q_sc[par * rp:(par + 1) * rp, :], k_ref[pl.ds(par, lanes, stride=2), :]) for par in range(2)],
            axis=0)

    def update(sc, v_list):
        m_prev = m_sc[...]
        m_new = jnp.maximum(m_prev, jnp.max(sc, axis=-1, keepdims=True))
        alpha = jnp.exp2(m_prev - m_new)
        p = jnp.exp2(sc - m_new)
        l_sc[...] = alpha * l_sc[...] + jnp.sum(p, axis=-1, keepdims=True)
        pv, off = None, 0
        for vr in v_list:
            n = vr.shape[0]
            part = _dot(p[:, off:off + n], vr[...])
            pv = part if pv is None else pv + part
            off += n
        acc_sc[...] = alpha * acc_sc[...] + pv
        m_sc[...] = m_new

    bias = bias_sc[...]
    update(jnp.concatenate([scores(kr, page_lanes) + bias for kr in k_refs], axis=1), v_refs)

    @pl.when(step_id == pl.num_programs(1) - 1)
    def _():
        sc = scores(kn_ref, vn_ref.shape[0])
        ok = own_head(sc.shape) & (_iota(sc.shape, 1) // heads <= _iota(sc.shape, 0) % half)
        update(jnp.where(ok, sc, NEG_INF), [vn_ref])
        lam = _lambda(lq1_ref, lk1_ref, lq2_ref, lk2_ref, lam_init)
        accn = acc_sc[...] / l_sc[...]
        y = _subln(accn[:rp] - lam * accn[rp:], g_ref[...], lam_init)
        for j in range(heads // 2):
            y8 = y[j * SUBLANES:(j + 1) * SUBLANES]
            o_ref[:, (2 * j) * dv:(2 * j + 1) * dv] = y8
            o_ref[:, (2 * j + 1) * dv:(2 * j + 2) * dv] = pltpu.roll(y8, half, 0)


def _sample_attention(q, k_new, v_new, cache_k, cache_v, pool_offset, page_table, lam_vecs, g, lam_init, n_new):
    b, pad_t, width = q.shape
    n_pages = page_table.shape[1]
    dk, dv = cache_k.shape[2], cache_v.shape[2]
    maps = width // dk
    heads = maps // 2
    page = cache_k.shape[1] // maps
    half = SUBLANES // 2
    assert n_new <= half and pad_t >= SUBLANES and heads % 2 == 0
    pages = math.gcd(n_pages, PAGES_PER_STEP)
    rows = 2 * heads * half
    vec = pl.BlockSpec((1, dk), lambda bi, s, pt: (0, 0))
    per_row = lambda a: pl.BlockSpec((None,) + a.shape[1:], lambda bi, s, pt: (bi, 0, 0))

    def page_spec(pg, a):
        return pl.BlockSpec((None,) + a.shape[1:], lambda bi, s, pt: (pool_offset + pt[bi, s * pages + pg], 0, 0))

    grid_spec = pltpu.PrefetchScalarGridSpec(
        num_scalar_prefetch=1,
        grid=(b, n_pages // pages),
        in_specs=([per_row(q), per_row(k_new), per_row(v_new)] + [page_spec(pg, cache_k) for pg in range(pages)]
                  + [page_spec(pg, cache_v) for pg in range(pages)]
                  + [vec, vec, vec, vec, pl.BlockSpec((1, dv), lambda bi, s, pt: (0, 0))]),
        out_specs=pl.BlockSpec((None, SUBLANES, heads * dv), lambda bi, s, pt: (bi, 0, 0)),
        scratch_shapes=[pltpu.VMEM((rows, dk), F32), pltpu.VMEM((rows, page * heads), F32),
                        pltpu.VMEM((rows, 1), F32), pltpu.VMEM((rows, 1), F32), pltpu.VMEM((rows, dv), F32)],
    )
    return pl.pallas_call(
        functools.partial(_sample_attn_kernel, lam_init=lam_init, heads=heads, dk=dk, dv=dv, pages=pages),
        grid_spec=grid_spec,
        out_shape=jax.ShapeDtypeStruct((b, SUBLANES, heads * dv), F32),
        compiler_params=_cparams(("parallel", "arbitrary")),
        name="sample_attention",
    )(page_table, q, k_new, v_new, *([cache_k] * pages), *([cache_v] * pages), *lam_vecs, g.reshape(1, dv))


def _segsums(xs, ones_bd):
    parts = []
    for x in xs:
        hi = x.astype(BF16)
        parts += [hi, (x - hi.astype(F32)).astype(BF16)]
    s = jnp.dot(jnp.concatenate(parts, axis=0), ones_bd, preferred_element_type=F32)
    m = xs[0].shape[0]
    return [s[2 * i * m:(2 * i + 1) * m] + s[(2 * i + 1) * m:(2 * i + 2) * m] for i in range(len(xs))]


def _segsum(x, ones_bd):
    return _segsums([x], ones_bd)[0]


def _block_ones(w):
    return jnp.where(_iota((w, w), 0) // RW_N == _iota((w, w), 1) // RW_N, 1.0, 0.0).astype(BF16)


def _rw_prep_kernel(z_ref, zprev_ref, mu_ref, w0_ref, w2_ref, a0_ref, a2_ref, g2_ref, kk_ref, ka_ref,
                    r_out, lw_out, k_out, v_out, kk_out, b_out, g_out, carry_sc, *, mode, period, rw, dd, da, gp):
    z = z_ref[...]
    tm = z.shape[0]
    rolled = pltpu.roll(z, 1, 0)
    row = _iota(z.shape, 0)
    if mode == "carry":
        @pl.when(pl.program_id(0) == 0)
        def _():
            carry_sc[...] = zprev_ref[...]
        zs = jnp.where(row == 0, carry_sc[...], rolled)
        carry_sc[...] = z[tm - 1:tm, :]
    else:
        zs = jnp.where(row % period == 0, zprev_ref[...], rolled)
    zz = z + (zs - z) * mu_ref[...]
    r = zz[:, :rw]
    kr = zz[:, rw:2 * rw]
    v = zz[:, 2 * rw:3 * rw]
    zw = zz[:, 3 * rw:3 * rw + dd]
    za = zz[:, 3 * rw + dd:3 * rw + dd + da]
    zg = zz[:, 3 * rw + dd + da:3 * rw + dd + da + gp]
    u = w0_ref[...] + _dot(jnp.tanh(zw), w2_ref[...])
    w_log = jnp.minimum(u, 0.0) - jnp.log(1.0 + jnp.exp(-jnp.abs(u))) - 0.5
    lw_out[...] = -jnp.exp(w_log)
    a = 1.0 / (1.0 + jnp.exp(-(a0_ref[...] + _dot(za, a2_ref[...]))))
    g_out[...] = _dot(1.0 / (1.0 + jnp.exp(-zg)), g2_ref[...])
    ones_bd = _block_ones(LANES)
    kk = kr * kk_ref[...]
    for c in range(rw // LANES):
        sl = slice(c * LANES, (c + 1) * LANES)
        kc = kk[:, sl]
        kc = kc * lax.rsqrt(jnp.maximum(_segsum(kc * kc, ones_bd), 1e-24))
        kk_out[:, sl] = kc
        b_out[:, sl] = kc * a[:, sl]
    r_out[...] = r
    v_out[...] = v
    k_out[...] = kr * (1.0 + (a - 1.0) * ka_ref[...])


def _rw_prep(z, zprev, prm, mode, period):
    m, zp = z.shape
    rw, dd, da, gp = prm["rw"], prm["dd"], prm["da"], prm["gp"]
    tm = _tile(m, 128)
    row = pl.BlockSpec((tm, zp), lambda i: (i, 0))
    full = lambda a: pl.BlockSpec(a.shape, lambda i: (0,) * a.ndim)
    zprev_spec = full(zprev) if mode == "carry" else row
    params = [prm["mu"], prm["w0"], prm["w2"], prm["a0"], prm["a2"], prm["g2"], prm["k_k"], prm["k_a"]]
    out_spec = pl.BlockSpec((tm, rw), lambda i: (i, 0))
    return pl.pallas_call(
        functools.partial(_rw_prep_kernel, mode=mode, period=period, rw=rw, dd=dd, da=da, gp=gp),
        grid=(m // tm,),
        in_specs=[row, zprev_spec] + [full(p) for p in params],
        out_specs=[out_spec] * 7,
        out_shape=[jax.ShapeDtypeStruct((m, rw), F32)] * 7,
        scratch_shapes=[pltpu.VMEM((1, zp), F32)],
        compiler_params=_cparams(("arbitrary",)),
        name="rwkv_prep",
    )(z, zprev, *params)


def _bd(x, gw):
    head = (_iota(x.shape, 1) % gw) // RW_N
    return jnp.concatenate([jnp.where(head == h, x, 0.0) for h in range(gw // RW_N)], axis=0)


def _rw_scan_kernel(*refs, has_s0, has_cast, t_valid, gw, n_groups):
    if has_cast:
        *refs, cast_in, y_out, s_out, cast_out, sb_sc = refs
        refs = refs + [y_out, s_out, sb_sc]
        cast_out[...] = cast_in[...].astype(cast_out.dtype)
    if has_s0:
        (r_ref, lw_ref, k_ref, v_ref, kk_ref, b_ref, g_ref, rk_ref, lng_ref, lnb_ref, s0_ref,
         y_out, s_out, sb_sc) = refs
    else:
        (r_ref, lw_ref, k_ref, v_ref, kk_ref, b_ref, g_ref, rk_ref, lng_ref, lnb_ref,
         y_out, s_out, sb_sc) = refs
    c = RW_CHUNK
    n = RW_N
    tc = r_ref.shape[0]
    ci = pl.program_id(1)
    last = pl.num_programs(1) - 1

    tri_inc = jnp.where(_iota((c, c), 1) <= _iota((c, c), 0), 1.0, 0.0).astype(BF16)
    s_idx = _iota((c, gw), 1) % n
    t_idx = _iota((c, gw), 0)
    strict = s_idx < t_idx
    incl = s_idx <= t_idx
    eye_w = jnp.where(s_idx == t_idx, 1.0, 0.0)
    gi0, gi1 = _iota((gw, gw), 0), _iota((gw, gw), 1)
    blk = gi0 // n == gi1 // n
    eye_g = jnp.where(gi0 == gi1, 1.0, 0.0)
    ones_bd = jnp.where(blk, 1.0, 0.0).astype(BF16)
    sel = jnp.where(_iota((gw, n), 0) % n == _iota((gw, n), 1), 1.0, 0.0).astype(BF16)
    sel_t = jnp.where(_iota((n, gw), 1) % n == _iota((n, gw), 0), 1.0, 0.0).astype(BF16)
    valid = _iota((tc, gw), 0) < t_valid

    def load(ref, sl, fill=0.0):
        x = ref[:, sl]
        if t_valid < tc:
            x = jnp.where(valid, x, fill)
        if tc < c:
            x = jnp.concatenate([x, jnp.full((c - tc, gw), fill, F32)], axis=0)
        return x

    @pl.when(ci == 0)
    def _():
        for gi in range(n_groups):
            if has_s0:
                sb_sc[gi] = jnp.where(blk, _dot_sel_r(s0_ref[gi * gw:(gi + 1) * gw, :], sel_t), 0.0)
            else:
                sb_sc[gi] = jnp.zeros((gw, gw), F32)

    def group_chain(gi):
        sl = slice(gi * gw, (gi + 1) * gw)
        r, lw, k, v, kk, b = (load(ref, sl) for ref in (r_ref, lw_ref, k_ref, v_ref, kk_ref, b_ref))
        cl = _dot_sel(tri_inc, lw)
        yield
        cend = cl[c - 1:c, :]
        kt = kk * jnp.exp(cl - lw)
        rt = r * jnp.exp(cl)
        g_inv = jnp.exp(-cl)
        g_rat = jnp.exp(cend - cl)
        ktil, btil = k * g_inv, b * g_inv
        khat, bhat = k * g_rat, b * g_rat
        la = _dot_nt(jnp.concatenate([kt, rt], axis=0),
                     jnp.concatenate([_bd(btil, gw), _bd(ktil, gw)], axis=0))
        yield
        lb = jnp.where(strict, la[:c, :gw], 0.0)
        lk = jnp.where(strict, la[:c, gw:], 0.0)
        ab = jnp.where(incl, la[c:, :gw], 0.0)
        ak = jnp.where(incl, la[c:, gw:], 0.0)
        lav = _dot(jnp.concatenate([lk, ak], axis=0), _bd(v, gw))
        lkv, y0 = lav[:c], lav[c:]
        qt = _dot_tn(v, khat)
        x = eye_w - lb
        rounds = max(0, math.ceil(math.log2(t_valid)) - 1)
        if rounds:
            p = _dot(lb, _bd(lb, gw))
            yield
        for rnd in range(rounds):
            if rnd == rounds - 1:
                x = x + _dot(x, _bd(p, gw))
            else:
                px = _dot(jnp.concatenate([p, x], axis=0), _bd(p, gw))
                p, x = px[:c], x + px[c:]
            yield
        tw = _dot(x, _bd(jnp.concatenate([kt, lkv], axis=1), gw))
        yield
        abw = _dot(ab, _bd(tw, gw))
        tn = _dot_tn(tw, bhat)
        yield
        rr = rt - abw[:, :gw]
        y0 = y0 - abw[:, gw:]
        pt = eye_g * jnp.exp(cend) - jnp.where(blk, tn[:gw], 0.0)
        qt = jnp.where(blk, qt - tn[gw:], 0.0)
        s_old = sb_sc[gi]
        y = _dot_nt(rr, s_old) + y0
        sb_sc[gi] = _dot(s_old, pt) + qt
        yield
        y_sum, rk_sum = _segsums([y, r * k * rk_ref[:, sl]], ones_bd)
        bonus = rk_sum * v
        yield
        d = y - y_sum * (1.0 / n)
        var = _segsum(d * d, ones_bd) * (1.0 / n)
        yield
        yn = d * lax.rsqrt(var + RW_LN_EPS) * lng_ref[:, sl] + lnb_ref[:, sl]
        out = (yn + bonus)[:tc] * g_ref[:, sl]
        y_out[:, sl] = out.astype(y_out.dtype)

    chains = [group_chain(gi) for gi in range(n_groups)]
    while chains:
        chains = [ch for ch in chains if next(ch, "done") != "done"]

    @pl.when(ci == last)
    def _():
        for gi in range(n_groups):
            s_out[gi * gw:(gi + 1) * gw, :] = _dot_sel_r(sb_sc[gi], sel)


def _rw_scan(r, lw, k, v, kk, b, g, prm, s0, n_seq, tc, t_valid, cast_src=None):
    m, rw = r.shape
    n_chunks = m // (n_seq * tc)
    gw = math.gcd(rw, RW_GROUP)
    n_groups = rw // gw
    row = pl.BlockSpec((tc, rw), lambda s, ci: (s * n_chunks + ci, 0))
    vec = pl.BlockSpec((1, rw), lambda s, ci: (0, 0))
    state = pl.BlockSpec((None, rw, RW_N), lambda s, ci: (s, 0, 0))
    has_s0 = s0 is not None
    ins = [r, lw, k, v, kk, b, g, prm["r_k"], prm["ln_g"], prm["ln_b"]] + ([s0] if has_s0 else [])
    in_specs = [row] * 7 + [vec] * 3 + ([state] if has_s0 else [])
    out_specs = [row, state]
    out_shape = [jax.ShapeDtypeStruct((m, rw), BF16), jax.ShapeDtypeStruct((n_seq, rw, RW_N), F32)]
    if cast_src is not None:
        src, layer = cast_src
        steps = n_seq * n_chunks
        assert src.shape[1] % steps == 0
        slab = src.shape[1] // steps
        in_specs.append(pl.BlockSpec((None, slab, src.shape[2]),
                                     lambda s, ci, layer=layer: (layer, s * n_chunks + ci, 0)))
        ins.append(src)
        out_specs.append(pl.BlockSpec((slab, src.shape[2]), lambda s, ci: (s * n_chunks + ci, 0)))
        out_shape.append(jax.ShapeDtypeStruct(src.shape[1:], BF16))
    out = pl.pallas_call(
        functools.partial(_rw_scan_kernel, has_s0=has_s0, has_cast=cast_src is not None, t_valid=t_valid, gw=gw,
                          n_groups=n_groups),
        grid=(n_seq, n_chunks),
        in_specs=in_specs,
        out_specs=out_specs,
        out_shape=out_shape,
        scratch_shapes=[pltpu.VMEM((n_groups, gw, gw), F32)],
        compiler_params=_cparams(("arbitrary", "arbitrary")),
        name="rwkv_scan",
    )(*ins)
    return (out[0], out[1], out[2][None]) if cast_src is not None else (out[0], out[1])


def _rope_tables(pos, dk):
    half = dk // 2
    inv = ROPE_THETA ** (-jnp.arange(half, dtype=F32) / half)
    ang = pos.astype(F32)[:, None] * inv[None, :]
    cos, sin = jnp.cos(ang), jnp.sin(ang)
    return jnp.concatenate([cos, cos], axis=-1), jnp.concatenate([-sin, sin], axis=-1)


def _prep_weights(i, dims, w_in, rw_mu, rw_w0, rw_w2, rw_a0, rw_a2, rw_g2, rw_k_k, rw_k_a, rw_r_k, rw_ln_g,
                  rw_ln_b):
    qw, da_w, rw, dd, da, dg, gp, zp = (dims[n] for n in ("qw", "da_w", "rw", "dd", "da", "dg", "gp", "zp"))
    rw_cols = 3 * rw + dd + da + dg
    w = w_in[i].astype(BF16)[None]
    prm = {
        "rw": rw, "dd": dd, "da": da, "gp": gp,
        "mu": jnp.pad(rw_mu[i], (0, zp - rw_cols)).reshape(1, zp),
        "w0": rw_w0[i].reshape(1, rw), "w2": rw_w2[i].astype(BF16),
        "a0": rw_a0[i].reshape(1, rw), "a2": rw_a2[i].astype(BF16),
        "g2": jnp.pad(rw_g2[i], ((0, gp - dg), (0, 0))).astype(BF16),
        "k_k": rw_k_k[i].reshape(1, rw), "k_a": rw_k_a[i].reshape(1, rw),
        "r_k": rw_r_k[i].reshape(1, rw), "ln_g": rw_ln_g[i].reshape(1, rw), "ln_b": rw_ln_b[i].reshape(1, rw),
    }
    return w, prm


def _tail(i, xs, os_, rwos, pes, w_out_b, w_up_b, w_down, w_ple_gate, w_ple_proj, gains):
    g_post_mix, g_pre_ffn, g_post_ffn, g_pre_ple, g_post_ple = gains
    d = xs[0].shape[1]
    d_ff = w_up_b.shape[2]
    half = os_[0].shape[1]
    assert rwos[0].shape[1] == half and w_out_b.shape[1] == 2 * half
    tn = _tile(d, MM_TILE)
    mix = _dense(_mm2_fn, [(os_[0], os_[1], False), (rwos[0], rwos[1], False)],
                 [(w_out_b, 0, half, 0, 0), (w_out_b, 0, half, 1, 0)], [(F32, None, F32)], n=d, tn=tn,
                 tm=MM_TILE // 2, name="out_proj")
    x, h = zip(*(_norm_residual(f[0], xx, g_post_mix, g_pre_ffn) for f, xx in zip(mix, xs)))
    tk = _tile(d_ff, FFN_TK)
    (u,), _, w_down_b = _dense(_ffn_up_fn, [(h[0], None, False)], [(w_up_b, 0, d, 0, 0)], [(BF16, None, None)],
                               n=d_ff, tn=_tile(d_ff, MM_TILE), cast_src=(w_down, i), name="ffn_up")
    (u_s,), _ = _dense(_ffn_up_fn, [(h[1], None, False)], [(w_up_b, 0, d, 0, 0)], [(BF16, None, None)],
                       n=d_ff, tn=_tile(d_ff, MM_TILE), name="ffn_up")
    f_p, _, w_gate_b = _dense(_mm_acc_fn, [(u, None, True)], [(w_down_b, 0, tk, 0, 0)], [(F32, None, None)],
                              n=d, tn=tn, k_steps=d_ff // tk, cast_src=(w_ple_gate, i), name="ffn_down")
    f_s, _ = _dense(_mm_acc_fn, [(u_s, None, True)], [(w_down_b, 0, tk, 0, 0)], [(F32, None, None)],
                    n=d, tn=tn, k_steps=d_ff // tk, name="ffn_down")
    f = [f_p, f_s]
    x, h = zip(*(_norm_residual(ff[0], xx, g_post_ffn, g_pre_ple) for ff, xx in zip(f, x)))
    e = _dense(_ple_fn, [(h[0], h[1], False), (pes[0], pes[1], False)],
               [(w_gate_b, 0, d, 0, 0), (w_ple_proj.astype(BF16), i, pes[0].shape[1], 0, 0)], [(F32, None, F32)],
               n=d, tn=tn, tm=MM_TILE // 2, name="ple")
    return tuple(_norm_residual(ee[0], xx, g_post_ple, None)[0] for ee, xx in zip(e, x))


def kernel(x_prompt, x_sample, cache_k, cache_v, state_wkv, state_shift, page_table, p_prompt, p_sample, g_pre_mix, w_in, lam_q1, lam_k1, lam_q2, lam_k2, da_subln_g, rw_mu, rw_w0, rw_w2, rw_a0, rw_a2, rw_g2, rw_k_k, rw_k_a, rw_r_k, rw_ln_g, rw_ln_b, w_out, g_post_mix, g_pre_ffn, w_up, w_down, g_post_ffn, g_pre_ple, w_ple_gate, w_ple_proj, g_post_ple):
    batch, seq, d = x_prompt.shape
    db, dseq, _ = x_sample.shape
    depth, n_pool, page, maps, dk = cache_k.shape
    heads, dv = cache_v.shape[3], cache_v.shape[4]
    rwh, rwn = rw_r_k.shape[1], rw_r_k.shape[2]
    assert maps == 2 * heads and dk == LANES and rwn == RW_N and seq % RW_CHUNK == 0
    qw, da_w, rw = maps * dk, heads * dv, rwh * rwn
    dd, da, dg = rw_w2.shape[1], rw_a2.shape[1], rw_g2.shape[1]
    assert dd % LANES == 0 and da % LANES == 0 and rw % LANES == 0
    gp = _round_up(dg, LANES)
    rw_cols = 3 * rw + dd + da + dg
    zp = _round_up(3 * rw + dd + da + gp, min(MM_TILE, _round_up(rw_cols, LANES)))
    dims = dict(qw=qw, da_w=da_w, rw=rw, dd=dd, da=da, dg=dg, gp=gp, zp=zp)
    n_pages = page_table.shape[1]
    past_len = n_pages * page
    pad_t = _round_up(dseq, SUBLANES)
    pad_a = _round_up(dseq, 2 * SUBLANES)
    scale = dk ** -0.5

    cos_p, sin_p = _rope_tables(jnp.tile(jnp.arange(seq, dtype=jnp.int32), batch), dk)
    cos_s, sin_s = _rope_tables(jnp.tile(past_len + jnp.arange(dseq, dtype=jnp.int32), db), dk)

    xp = x_prompt.reshape(batch * seq, d)
    xs = x_sample.reshape(db * dseq, d)
    outs = [[] for _ in range(8)]
    for i in range(depth):
        lam_init = 0.8 - 0.6 * math.exp(-0.3 * i)
        w, prm = _prep_weights(i, dims, w_in, rw_mu, rw_w0, rw_w2, rw_a0, rw_a2, rw_g2, rw_k_k, rw_k_a, rw_r_k,
                               rw_ln_g, rw_ln_b)
        lam_vecs = [v[i].reshape(1, dk) for v in (lam_q1, lam_k1, lam_q2, lam_k2)]
        gains = (g_post_mix[i], g_pre_ffn[i], g_post_ffn[i], g_pre_ple[i], g_post_ple[i])

        h = _rmsnorm(xp, g_pre_mix[i], NORM_EPS)
        h_s = _rmsnorm(xs, g_pre_mix[i], NORM_EPS)
        roped = [(h, h_s, False), (cos_p, cos_s, False), (sin_p, sin_s, False)]
        tn_q, tn_z = _tile(qw, MM_TILE), _tile(zp, MM_TILE)
        assert qw % tn_q == 0 and (2 * qw) % da_w == 0 and (2 * qw + da_w) % tn_z == 0
        (q,), (q_s,) = _dense(
            functools.partial(_proj_fn, rope=True, scale=scale * math.log2(math.e), hw=dk),
            roped, [(w, 0, d, 0, 0)], [(BF16, None, F32)], n=qw, tn=tn_q, name="proj_q")
        (k32, k16), (k32_s,) = _dense(
            functools.partial(_proj_fn, rope=True, scale=1.0, hw=dk),
            roped, [(w, 0, d, 0, qw // tn_q)], [(F32, dk, F32), (BF16, None, None)], n=qw, tn=tn_q,
            tm=MM_TILE // 2, name="proj_k")
        v_fn = functools.partial(_proj_fn, rope=False, scale=1.0, hw=dv)
        v_w = [(w, 0, d, 0, 2 * qw // da_w)]
        (v32, v16), _ = _dense(v_fn, [(h, None, False)], v_w, [(F32, dv, None), (BF16, None, None)], n=da_w,
                               tn=da_w, tm=MM_TILE // 2, name="proj_v")
        (v32_s,), _ = _dense(v_fn, [(h_s, None, False)], v_w, [(F32, None, None)], n=da_w, tn=da_w, name="proj_v")
        (z,), (z_s,) = _dense(functools.partial(_proj_fn, rope=False, scale=1.0, hw=tn_z, n_valid=rw_cols),
                              [(h, h_s, False)],
                              [(w, 0, d, 0, (2 * qw + da_w) // tn_z)], [(F32, None, F32)], n=zp, tn=tn_z,
                              name="proj_z")

        o_rows, rw_rows, wkv_rows = [], [], []
        for bi in range(batch):
            rs = slice(bi * seq, (bi + 1) * seq)
            o_rows.append(_prompt_attention(q[rs], k16[rs], v16[rs], lam_vecs, da_subln_g[i], lam_init,
                                            heads, dk, dv))
            mixer_in = _rw_prep(z[rs], jnp.zeros((1, zp), F32), prm, "carry", 0)
            if bi == 0:
                rwo, s_fin, w_up_b = _rw_scan(*mixer_in, prm, None, 1, RW_CHUNK, RW_CHUNK, cast_src=(w_up, i))
            else:
                rwo, s_fin = _rw_scan(*mixer_in, prm, None, 1, RW_CHUNK, RW_CHUNK)
            rw_rows.append(rwo)
            wkv_rows.append(s_fin.reshape(rwh, rwn, rwn))
        o = o_rows[0] if batch == 1 else jnp.concatenate(o_rows, axis=0)
        rwo = rw_rows[0] if batch == 1 else jnp.concatenate(rw_rows, axis=0)
        outs[0].append(k32.reshape(batch, seq, maps, dk))
        outs[1].append(v32.reshape(batch, seq, heads, dv))
        outs[4].append(jnp.stack(wkv_rows, 0))
        outs[6].append(z.reshape(batch, seq, zp)[:, -1, :rw_cols])

        pad3 = lambda a, t: jnp.pad(a.reshape(db, dseq, -1), ((0, 0), (0, t - dseq), (0, 0)))
        o_s = _sample_attention(pad3(q_s, pad_a), pad3(k32_s, pad_a).reshape(db, pad_a * maps, dk),
                                pad3(v32_s, pad_a).reshape(db, pad_a * heads, dv),
                                cache_k.reshape(depth * n_pool, page * maps, dk),
                                cache_v.reshape(depth * n_pool, page * heads, dv), i * n_pool, page_table,
                                lam_vecs, da_subln_g[i], lam_init, dseq)
        o_s = o_s[:, :dseq].reshape(db * dseq, da_w).astype(BF16)
        zprev = jnp.zeros((db, pad_t, zp), F32).at[:, 0, :rw_cols].set(state_shift[i])
        mixer_in = _rw_prep(pad3(z_s, pad_t).reshape(db * pad_t, zp), zprev.reshape(db * pad_t, zp), prm, "rows",
                            pad_t)
        rwo_s, s_fin, w_out_b = _rw_scan(*mixer_in, prm, state_wkv[i].reshape(db, rw, rwn), db, pad_t, dseq,
                                         cast_src=(w_out, i))
        rwo_s = rwo_s.reshape(db, pad_t, rw)[:, :dseq].reshape(db * dseq, rw)
        outs[2].append(k32_s.reshape(db, dseq, maps, dk))
        outs[3].append(v32_s.reshape(db, dseq, heads, dv))
        outs[5].append(s_fin.reshape(db, rwh, rwn, rwn))
        outs[7].append(z_s.reshape(db, dseq, zp)[:, -1, :rw_cols])

        xp, xs = _tail(i, (xp, xs), (o, o_s), (rwo, rwo_s),
                       (p_prompt[i].reshape(batch * seq, -1), p_sample[i].reshape(db * dseq, -1)),
                       w_out_b, w_up_b, w_down, w_ple_gate, w_ple_proj, gains)

    st =[jnp.stack(o, 0) for o in outs]
    return (xp.reshape(batch, seq, d), xs.reshape(db, dseq, d), st[0], st[1], st[2], st[3], st[4], st[5],
            st[6], st[7])
```
